```python
import math
import jax, jax.numpy as jnp
from jax import lax
import numpy as np

D_MODEL = 2048
BATCH = 2
SEQ = 16384
DEPTH = 1

D_RNN = 128 * ((4 * D_MODEL // 3) // 128)
LRU_BLOCKS = 16
LRU_BLOCK = D_RNN // LRU_BLOCKS
LRU_C = 8.0
CONV_REC = 4
ATT_GROUPS = ((128, 1), (512, 4), (2048, 16))
N_GROUPS = 3
HEADS_PER_GROUP = 8
N_ATT_HEADS = N_GROUPS * HEADS_PER_GROUP
HEAD_DIM = 128
ATT_WIDTH = N_ATT_HEADS * HEAD_DIM
ATT_OUT = HEADS_PER_GROUP * HEAD_DIM
Q_BLOCK = 128
D_FF = 256 * ((8 * D_MODEL // 3 + 255) // 256)
CONV_FFN = 3
OFF_GATE_REC = D_RNN
OFF_Q = 2 * D_RNN
OFF_K = OFF_Q + ATT_WIDTH
OFF_V = OFF_K + ATT_WIDTH
OFF_G = OFF_V + ATT_WIDTH
N_IN = OFF_G + 2 * D_MODEL
EPS = 1e-6

kernel_name = "hybrid_rglru_dilated_alibi_convffn"


def rms_norm(x, g):
    xf = x.astype(jnp.float32)
    y = xf * lax.rsqrt(jnp.mean(xf * xf, axis=-1, keepdims=True) + EPS) * g.astype(jnp.float32)
    return y.astype(x.dtype)


def causal_dwconv(x, w, b):
    K, C = w.shape
    y = lax.conv_general_dilated(
        x, w[:, None, :].astype(x.dtype), window_strides=(1,), padding=[(K - 1, 0)],
        dimension_numbers=('NWC', 'WIO', 'NWC'), feature_group_count=C)
    return y + b.astype(x.dtype)


def alibi_slopes(n):
    def pow2_slopes(m):
        start = 2.0 ** (-8.0 / m)
        return [start ** (i + 1) for i in range(m)]
    c = 2 ** int(math.floor(math.log2(n)))
    s = pow2_slopes(c) + pow2_slopes(2 * c)[0::2][: n - c]
    return np.sort(np.asarray(s, np.float32))[::-1].copy()


def rg_lru_branch(x_in, gate_in, conv_w, conv_b, wr, br, wi, bi, lam):
    B, S, _ = x_in.shape
    xf = causal_dwconv(x_in, conv_w, conv_b).astype(jnp.float32)
    xb = xf.reshape(B, S, LRU_BLOCKS, LRU_BLOCK)
    r = jax.nn.sigmoid(jnp.einsum('bsnc,ncd->bsnd', xb, wr.astype(jnp.float32)).reshape(B, S, D_RNN) + br)
    i = jax.nn.sigmoid(jnp.einsum('bsnc,ncd->bsnd', xb, wi.astype(jnp.float32)).reshape(B, S, D_RNN) + bi)
    log_a = -LRU_C * r * jax.nn.softplus(-lam.astype(jnp.float32))
    a = jnp.exp(log_a)
    b = jnp.sqrt(-jnp.expm1(2.0 * log_a)) * (i * xf)

    def combine(left, right):
        a_l, b_l = left
        a_r, b_r = right
        return a_l * a_r, a_r * b_l + b_r

    _, h = lax.associative_scan(combine, (a, b), axis=1)
    return h * jax.nn.gelu(gate_in.astype(jnp.float32))


def dilated_window_attention(q, k, v, slopes, window, dilation):
    B, S, H, hd = q.shape
    span = window // dilation
    chunk = dilation * Q_BLOCK
    s_pad = -(-S // chunk) * chunk
    L = s_pad // dilation
    nb = L // Q_BLOCK
    seq_pad = ((0, 0), (0, s_pad - S), (0, 0), (0, 0))

    def to_sub(t):
        t = jnp.pad(t, seq_pad).reshape(B, L, dilation, H, hd)
        return t.transpose(0, 2, 1, 3, 4)

    def band(t):
        prev = jnp.pad(t, ((0, 0), (0, 0), (Q_BLOCK, 0), (0, 0), (0, 0)))[:, :, :L]
        return jnp.concatenate([prev.reshape(B, dilation, nb, Q_BLOCK, H, hd),
                                t.reshape(B, dilation, nb, Q_BLOCK, H, hd)], axis=3)

    qb = to_sub(q.astype(jnp.float32)).reshape(B, dilation, nb, Q_BLOCK, H, hd)
    kb = band(to_sub(k.astype(jnp.float32)))
    vb = band(to_sub(v.astype(jnp.float32)))

    scores = jnp.einsum('bdnqhe,bdnkhe->bdnhqk', qb, kb) * (hd ** -0.5)
    qi = np.arange(Q_BLOCK)[:, None]
    ki = np.arange(2 * Q_BLOCK)[None, :]
    steps = qi + Q_BLOCK - ki
    in_band = (steps >= 0) & (steps <= span)
    u_key = np.arange(nb)[:, None, None] * Q_BLOCK + ki[None] - Q_BLOCK
    valid = in_band[None] & (u_key >= 0)
    bias = -slopes[:, None, None] * (steps * dilation).astype(np.float32)[None]
    scores = jnp.where(valid[None, None, :, None], scores + bias[None, None, None], -jnp.inf)

    m = jnp.max(scores, axis=-1, keepdims=True)
    p = jnp.exp(scores - m)
    den = jnp.sum(p, axis=-1, keepdims=True)
    out = jnp.einsum('bdnhqk,bdnkhe->bdnhqe', p, vb) / den
    lse = (m + jnp.log(den))[..., 0]
    out = out.transpose(0, 1, 2, 4, 3, 5).reshape(B, dilation, L, H, hd)
    out = out.transpose(0, 2, 1, 3, 4).reshape(B, s_pad, H, hd)[:, :S]
    lse = lse.transpose(0, 1, 2, 4, 3).reshape(B, dilation, L, H)
    lse = lse.transpose(0, 2, 1, 3).reshape(B, s_pad, H)[:, :S]
    return out, lse


def setup_inputs(seed: int = 0) -> dict:
    key = jax.random.key(seed)
    ks = jax.random.split(key, 24)
    f32 = jnp.float32

    def nrm(k, shape, scale):
        return jax.random.normal(k, shape, f32) * scale

    a0 = jax.random.uniform(ks[9], (DEPTH, D_RNN), f32, minval=0.9, maxval=0.999)
    s0 = a0 ** (1.0 / LRU_C)
    lru_lambda = jnp.log(s0) - jnp.log1p(-s0)
    return {
        'x': nrm(ks[0], (BATCH, SEQ, D_MODEL), 1.0),
        'norm1_g': 1.0 + nrm(ks[1], (DEPTH, D_MODEL), 0.02),
        'w_in': nrm(ks[2], (DEPTH, D_MODEL, N_IN), D_MODEL ** -0.5),
        'conv_w': nrm(ks[3], (DEPTH, CONV_REC, D_RNN), CONV_REC ** -0.5),
        'conv_b': nrm(ks[4], (DEPTH, D_RNN), 0.02),
        'lru_wr': nrm(ks[5], (DEPTH, LRU_BLOCKS, LRU_BLOCK, LRU_BLOCK), LRU_BLOCK ** -0.5),
        'lru_br': nrm(ks[6], (DEPTH, D_RNN), 0.02),
        'lru_wi': nrm(ks[7], (DEPTH, LRU_BLOCKS, LRU_BLOCK, LRU_BLOCK), LRU_BLOCK ** -0.5),
        'lru_bi': nrm(ks[8], (DEPTH, D_RNN), 0.02),
        'lru_lambda': lru_lambda,
        'w_rnn_out': nrm(ks[10], (DEPTH, D_RNN, D_MODEL), D_RNN ** -0.5),
        'w_att_out': nrm(ks[11], (DEPTH, ATT_OUT, D_MODEL), ATT_OUT ** -0.5),
        'w_out': nrm(ks[12], (DEPTH, D_MODEL, D_MODEL), D_MODEL ** -0.5),
        'norm2_g': 1.0 + nrm(ks[13], (DEPTH, D_MODEL), 0.02),
        'w_up': nrm(ks[14], (DEPTH, D_MODEL, 2 * D_FF), D_MODEL ** -0.5),
        'ffn_conv_w': nrm(ks[15], (DEPTH, CONV_FFN, D_FF), CONV_FFN ** -0.5),
        'ffn_conv_b': nrm(ks[16], (DEPTH, D_FF), 0.02),
        'w_down': nrm(ks[17], (DEPTH, D_FF, D_MODEL), D_FF ** -0.5),
        'final_g': 1.0 + nrm(ks[18], (D_MODEL,), 0.02),
    }


def reference(x, norm1_g, w_in, conv_w, conv_b, lru_wr, lru_br, lru_wi, lru_bi, lru_lambda,
              w_rnn_out, w_att_out, w_out, norm2_g, w_up, ffn_conv_w, ffn_conv_b, w_down, final_g):
    B, S, _ = x.shape
    slopes = alibi_slopes(N_ATT_HEADS).reshape(N_GROUPS, HEADS_PER_GROUP)
    for l in range(DEPTH):
        h = rms_norm(x, norm1_g[l])
        w = w_in[l]
        x_rec = h @ w[:, :OFF_GATE_REC]
        gate_rec = h @ w[:, OFF_GATE_REC:OFF_Q]
        q = (h @ w[:, OFF_Q:OFF_K]).reshape(B, S, N_GROUPS, HEADS_PER_GROUP, HEAD_DIM)
        k = (h @ w[:, OFF_K:OFF_V]).reshape(B, S, N_GROUPS, HEADS_PER_GROUP, HEAD_DIM)
        v = (h @ w[:, OFF_V:OFF_G]).reshape(B, S, N_GROUPS, HEADS_PER_GROUP, HEAD_DIM)
        gates = jax.nn.sigmoid((h @ w[:, OFF_G:]).astype(jnp.float32))
        g_rec, g_att = gates[..., :D_MODEL], gates[..., D_MODEL:]

        y_rec = rg_lru_branch(x_rec, gate_rec, conv_w[l], conv_b[l], lru_wr[l], lru_br[l],
                              lru_wi[l], lru_bi[l], lru_lambda[l])
        y_a = y_rec.astype(x.dtype) @ w_rnn_out[l]

        outs, lses = [], []
        for g, (window, dilation) in enumerate(ATT_GROUPS):
            o, s = dilated_window_attention(q[:, :, g], k[:, :, g], v[:, :, g], slopes[g], window, dilation)
            outs.append(o)
            lses.append(s)
        wts = jax.nn.softmax(jnp.stack(lses, axis=0), axis=0)
        att = jnp.einsum('gbsh,gbshe->bshe', wts, jnp.stack(outs, axis=0))
        y_b = att.reshape(B, S, ATT_OUT).astype(x.dtype) @ w_att_out[l]

        mixed = g_rec * y_a.astype(jnp.float32) + g_att * y_b.astype(jnp.float32)
        x = x + mixed.astype(x.dtype) @ w_out[l]

        h2 = rms_norm(x, norm2_g[l])
        wu = w_up[l]
        gate = causal_dwconv(h2 @ wu[:, :D_FF], ffn_conv_w[l], ffn_conv_b[l])
        val = h2 @ wu[:, D_FF:]
        x = x + (jax.nn.gelu(gate) * val) @ w_down[l]
    return rms_norm(x, final_g)
```

```python
import functools
import math

import numpy as np
import jax
import jax.numpy as jnp
from jax import lax
from jax.experimental import pallas as pl
from jax.experimental.pallas import tpu as pltpu

LRU_BLOCKS = 16
LRU_C = 8.0
CONV_REC = 4
ATT_GROUPS = ((128, 1), (512, 4), (2048, 16))
HEADS_PER_GROUP = 8
HEAD_DIM = 128
Q_BLOCK = 128
CONV_FFN = 3
EPS = 1e-6

V7X_LANES = 128
V7X_SUBLANES = 8
V7X_MXU_DIM = 256
V7X_VMEM_BYTES = 64 * 1024 * 1024

F32 = jnp.float32
BF16 = jnp.bfloat16


def _params(n_axes, vmem_bytes):
    limit = min(int(vmem_bytes * 1.25) + (4 << 20), V7X_VMEM_BYTES - (4 << 20))
    return pltpu.CompilerParams(dimension_semantics=("arbitrary",) * n_axes, vmem_limit_bytes=limit)


def _rms(xf, g):
    return xf * lax.rsqrt(jnp.mean(xf * xf, axis=-1, keepdims=True) + EPS) * g


def _norm_body(x_ref, g_ref, o_ref):
    o_ref[...] = _rms(x_ref[...], g_ref[...]).astype(o_ref.dtype)


def _norm(x, g, *, tm):
    M, D = x.shape
    return pl.pallas_call(
        _norm_body,
        grid=(M // tm,),
        in_specs=[pl.BlockSpec((tm, D), lambda i: (i, 0)), pl.BlockSpec((1, D), lambda i: (0, 0))],
        out_specs=pl.BlockSpec((tm, D), lambda i: (i, 0)),
        out_shape=jax.ShapeDtypeStruct((M, D), BF16),
        compiler_params=_params(1, 2 * tm * D * 6),
        name="norm1",
    )(x, g)


def _mm_body(a_ref, w_ref, o_ref, *, act):
    acc = jnp.dot(a_ref[...], w_ref[...], preferred_element_type=F32)
    if act == "sigmoid":
        acc = jax.nn.sigmoid(acc)
    o_ref[...] = acc.astype(o_ref.dtype)


def _matmul(a, w, *, tm, tn, act, name):
    M, K = a.shape
    N = w.shape[1]
    assert M % tm == 0 and N % tn == 0
    vmem = 2 * (tm * K * 2 + K * tn * 2 + tm * tn * 2) + tm * tn * 4
    return pl.pallas_call(
        functools.partial(_mm_body, act=act),
        grid=(M // tm, N // tn),
        in_specs=[pl.BlockSpec((tm, K), lambda i, j: (i, 0)), pl.BlockSpec((K, tn), lambda i, j: (0, j))],
        out_specs=pl.BlockSpec((tm, tn), lambda i, j: (i, j)),
        out_shape=jax.ShapeDtypeStruct((M, N), BF16),
        compiler_params=_params(2, vmem),
        name=name,
    )(a, w)


def _lru_band(d_rnn, block):
    band = []
    for c0 in range(0, d_rnn, V7X_MXU_DIM):
        cn = min(V7X_MXU_DIM, d_rnn - c0)
        b0, b1 = c0 // block, (c0 + cn - 1) // block
        r0 = (block * b0) // V7X_LANES * V7X_LANES
        r1 = min(-(-(block * (b1 + 1)) // V7X_LANES) * V7X_LANES, d_rnn)
        band.append((c0, cn, r0, r1 - r0))
    return tuple(band)


def _band_tiles(w, band):
    dense = jax.scipy.linalg.block_diag(*[w[n] for n in range(w.shape[0])])
    kmax = max(k for _, _, _, k in band)
    tiles = []
    for c0, cn, r0, k in band:
        t = dense[r0:r0 + k, c0:c0 + cn]
        tiles.append(jnp.pad(t, ((0, kmax - k), (0, V7X_MXU_DIM - cn))))
    return jnp.stack(tiles).astype(BF16)


def _shift_rows(v, s, fill):
    T, C = v.shape
    if s % V7X_SUBLANES == 0:
        return jnp.concatenate([jnp.full((s, C), fill, v.dtype), v[:T - s]], axis=0)
    rolled = pltpu.roll(v, s, axis=0)
    row = lax.broadcasted_iota(jnp.int32, v.shape, 0)
    return jnp.where(row < s, jnp.asarray(fill, v.dtype), rolled)


def _lru_body(x_ref, g_ref, cw_ref, cb_ref, wr_ref, wi_ref, br_ref, bi_ref, lam_ref, y_ref,
              xs_ref, xf_ref, xb_ref, hc_ref, *, ts, tiles_per_batch, band, conv_chunk):
    t = pl.program_id(0)
    d_rnn = x_ref.shape[1]

    @pl.when(t % tiles_per_batch == 0)
    def _():
        xs_ref[0:V7X_SUBLANES, :] = jnp.zeros((V7X_SUBLANES, d_rnn), F32)
        hc_ref[...] = jnp.zeros(hc_ref.shape, F32)

    xs_ref[V7X_SUBLANES:V7X_SUBLANES + ts, :] = x_ref[...].astype(F32)
    for c in range(0, d_rnn, conv_chunk):
        cs = slice(c, c + conv_chunk)
        acc = cb_ref[:, cs] + cw_ref[CONV_REC - 1:CONV_REC, cs] * xs_ref[V7X_SUBLANES:V7X_SUBLANES + ts, cs]
        for i in range(CONV_REC - 1):
            r0 = V7X_SUBLANES - (CONV_REC - 1) + i
            acc = acc + cw_ref[i:i + 1, cs] * xs_ref[r0:r0 + ts, cs]
        xf_ref[:, cs] = acc
        xb_ref[:, cs] = acc.astype(BF16)
    xs_ref[0:V7X_SUBLANES, :] = xs_ref[ts:ts + V7X_SUBLANES, :]

    for j, (c0, cn, r0, k) in enumerate(band):
        cs = slice(c0, c0 + cn)
        win = xb_ref[:, r0:r0 + k]
        zr = jnp.dot(win, wr_ref[j, :k, :], preferred_element_type=F32)[:, :cn] + br_ref[:, cs]
        zi = jnp.dot(win, wi_ref[j, :k, :], preferred_element_type=F32)[:, :cn] + bi_ref[:, cs]
        r = jax.nn.sigmoid(zr)
        gi = jax.nn.sigmoid(zi)
        log_a = -LRU_C * r * jax.nn.softplus(-lam_ref[:, cs])
        a = jnp.exp(log_a)
        b = jnp.sqrt(1.0 - jnp.exp(2.0 * log_a)) * (gi * xf_ref[:, cs])
        s = 1
        while s < ts:
            b = a * _shift_rows(b, s, 0.0) + b
            a = a * _shift_rows(a, s, 1.0)
            s *= 2
        h = a * hc_ref[0:1, cs] + b
        hc_ref[0:1, cs] = h[ts - 1:ts, :]
        y_ref[:, cs] = (h * jax.nn.gelu(g_ref[:, cs].astype(F32))).astype(y_ref.dtype)


def _lru(xg, conv_w, conv_b, wr_t, wi_t, br, bi, lam, *, seq, ts, band):
    M = xg.shape[0]
    d_rnn = xg.shape[1] // 2
    nt, kmax, _ = wr_t.shape
    row = lambda i: (0, 0)
    vmem = (2 * 3 * ts * d_rnn * 2 + 2 * 2 * nt * kmax * V7X_MXU_DIM * 2
            + (ts + V7X_SUBLANES) * d_rnn * 4 + ts * d_rnn * 6 + 16 * ts * V7X_MXU_DIM * 4)
    return pl.pallas_call(
        functools.partial(_lru_body, ts=ts, tiles_per_batch=seq // ts, band=band, conv_chunk=3 * V7X_LANES),
        grid=(M // ts,),
        in_specs=[
            pl.BlockSpec((ts, d_rnn), lambda i: (i, 0)),
            pl.BlockSpec((ts, d_rnn), lambda i: (i, 1)),
            pl.BlockSpec((CONV_REC, d_rnn), row),
            pl.BlockSpec((1, d_rnn), row),
            pl.BlockSpec((nt, kmax, V7X_MXU_DIM), lambda i: (0, 0, 0)),
            pl.BlockSpec((nt, kmax, V7X_MXU_DIM), lambda i: (0, 0, 0)),
            pl.BlockSpec((1, d_rnn), row),
            pl.BlockSpec((1, d_rnn), row),
            pl.BlockSpec((1, d_rnn), row),
        ],
        out_specs=pl.BlockSpec((ts, d_rnn), lambda i: (i, 0)),
        out_shape=jax.ShapeDtypeStruct((M, d_rnn), BF16),
        scratch_shapes=[
            pltpu.VMEM((ts + V7X_SUBLANES, d_rnn), F32),
            pltpu.VMEM((ts, d_rnn), F32),
            pltpu.VMEM((ts, d_rnn), BF16),
            pltpu.VMEM((V7X_SUBLANES, d_rnn), F32),
        ],
        compiler_params=_params(1, vmem),
        name="rglru",
    )(xg, xg, conv_w, conv_b, wr_t, wi_t, br, bi, lam)


def _alibi_slopes(n):
    def pow2_slopes(m):
        start = 2.0 ** (-8.0 / m)
        return [start ** (i + 1) for i in range(m)]
    c = 2 ** int(math.floor(math.log2(n)))
    s = pow2_slopes(c) + pow2_slopes(2 * c)[0::2][: n - c]
    return np.sort(np.asarray(s, np.float32))[::-1].copy()


def _attn_body(q_ref, kc_ref, kp_ref, vc_ref, vp_ref, o_ref, l_ref, *, dilation, span, slopes):
    n = pl.program_id(1)
    qi = lax.broadcasted_iota(jnp.int32, (Q_BLOCK, 2 * Q_BLOCK), 0)
    ki = lax.broadcasted_iota(jnp.int32, (Q_BLOCK, 2 * Q_BLOCK), 1)
    steps = qi + Q_BLOCK - ki
    first_key = jnp.where(n > 0, 0, Q_BLOCK)
    valid = (steps >= 0) & (steps <= span) & (ki >= first_key)
    dist = (steps * dilation).astype(F32)
    scale = HEAD_DIM ** -0.5
    for h in range(HEADS_PER_GROUP):
        hs = slice(h * HEAD_DIM, (h + 1) * HEAD_DIM)
        qh = q_ref[0, :, hs]
        kcat = jnp.concatenate([kp_ref[0, :, hs], kc_ref[0, :, hs]], axis=0)
        vcat = jnp.concatenate([vp_ref[0, :, hs], vc_ref[0, :, hs]], axis=0)
        s = lax.dot_general(qh, kcat, (((1,), (1,)), ((), ())), preferred_element_type=F32) * scale
        s = jnp.where(valid, s - float(slopes[h]) * dist, -jnp.inf)
        m = jnp.max(s, axis=-1, keepdims=True)
        p = jnp.exp(s - m)
        den = jnp.sum(p, axis=-1, keepdims=True)
        o = jnp.dot(p.astype(BF16), vcat, preferred_element_type=F32) / den
        o_ref[0, :, hs] = o
        l_ref[0, :, hs] = jnp.broadcast_to(m + jnp.log(den), (Q_BLOCK, HEAD_DIM))


def _attention_group(qkv, g, *, batch, seq, window, dilation, slopes):
    M, width = qkv.shape
    gw = HEADS_PER_GROUP * HEAD_DIM
    n_groups = width // (3 * gw)
    assert seq % (dilation * Q_BLOCK) == 0
    L = seq // dilation
    view = qkv.reshape(batch, L, dilation * width)
    per_tok = width // gw

    def col(sec):
        return lambda b, n, r: (b, n, r * per_tok + sec * n_groups + g)

    def col_prev(sec):
        return lambda b, n, r: (b, jnp.maximum(n - 1, 0), r * per_tok + sec * n_groups + g)

    blk = (1, Q_BLOCK, gw)
    out_shape = jax.ShapeDtypeStruct((batch, L, dilation * gw), F32)
    o, l = pl.pallas_call(
        functools.partial(_attn_body, dilation=dilation, span=window // dilation, slopes=slopes),
        grid=(batch, L // Q_BLOCK, dilation),
        in_specs=[pl.BlockSpec(blk, col(0)), pl.BlockSpec(blk, col(1)), pl.BlockSpec(blk, col_prev(1)),
                  pl.BlockSpec(blk, col(2)), pl.BlockSpec(blk, col_prev(2))],
        out_specs=[pl.BlockSpec(blk, lambda b, n, r: (b, n, r))] * 2,
        out_shape=[out_shape, out_shape],
        compiler_params=_params(3, 2 * (5 * Q_BLOCK * gw * 2 + 2 * Q_BLOCK * gw * 4) + (4 << 20)),
        name=f"attn_d{dilation}",
    )(view, view, view, view, view)
    return o.reshape(M, gw), l.reshape(M, gw)


def _mix_body(y_ref, o1_ref, l1_ref, o2_ref, l2_ref, o3_ref, l3_ref, gr_ref, ga_ref, wr_ref, wa_ref,
              out_ref, att_ref):
    @pl.when(pl.program_id(1) == 0)
    def _():
        l1, l2, l3 = l1_ref[...], l2_ref[...], l3_ref[...]
        m = jnp.maximum(jnp.maximum(l1, l2), l3)
        e1, e2, e3 = jnp.exp(l1 - m), jnp.exp(l2 - m), jnp.exp(l3 - m)
        den = e1 + e2 + e3
        att = (e1 / den) * o1_ref[...] + (e2 / den) * o2_ref[...] + (e3 / den) * o3_ref[...]
        att_ref[...] = att.astype(BF16)

    ya = jnp.dot(y_ref[...], wr_ref[...], preferred_element_type=F32)
    yb = jnp.dot(att_ref[...], wa_ref[...], preferred_element_type=F32)
    out_ref[...] = (gr_ref[...].astype(F32) * ya + ga_ref[...].astype(F32) * yb).astype(out_ref.dtype)


def _mix(y_rec, att_parts, gates, w_rnn, w_att, *, tm, tn):
    M, d_rnn = y_rec.shape
    aw = w_att.shape[0]
    D = w_rnn.shape[1]
    nj = D // tn
    rowblk = lambda w: pl.BlockSpec((tm, w), lambda i, j: (i, 0))
    vmem = (2 * (tm * d_rnn * 2 + 6 * tm * aw * 4 + 2 * tm * tn * 2 + (d_rnn + aw) * tn * 2 + tm * tn * 2)
            + tm * aw * 2 + 6 * tm * aw * 4)
    return pl.pallas_call(
        _mix_body,
        grid=(M // tm, nj),
        in_specs=[rowblk(d_rnn)] + [rowblk(aw)] * 6 + [
            pl.BlockSpec((tm, tn), lambda i, j: (i, j)),
            pl.BlockSpec((tm, tn), lambda i, j: (i, j + nj)),
            pl.BlockSpec((d_rnn, tn), lambda i, j: (0, j)),
            pl.BlockSpec((aw, tn), lambda i, j: (0, j)),
        ],
        out_specs=pl.BlockSpec((tm, tn), lambda i, j: (i, j)),
        out_shape=jax.ShapeDtypeStruct((M, D), BF16),
        scratch_shapes=[pltpu.VMEM((tm, aw), BF16)],
        compiler_params=_params(2, vmem),
        name="mix",
    )(y_rec, *att_parts, gates, gates, w_rnn, w_att)


def _outproj_body(m_ref, w_ref, x_ref, g_ref, x1_ref, h2_ref):
    x1 = x_ref[...] + jnp.dot(m_ref[...], w_ref[...], preferred_element_type=F32)
    x1_ref[...] = x1
    h2_ref[...] = _rms(x1, g_ref[...]).astype(h2_ref.dtype)


def _outproj(mixed, w_out, x, g2, *, tm):
    M, D = x.shape
    vmem = 2 * (tm * D * 2 + D * D * 2 + tm * D * 4 + tm * D * 4 + tm * D * 2) + 2 * tm * D * 4
    return pl.pallas_call(
        _outproj_body,
        grid=(M // tm,),
        in_specs=[pl.BlockSpec((tm, D), lambda i: (i, 0)), pl.BlockSpec((D, D), lambda i: (0, 0)),
                  pl.BlockSpec((tm, D), lambda i: (i, 0)), pl.BlockSpec((1, D), lambda i: (0, 0))],
        out_specs=[pl.BlockSpec((tm, D), lambda i: (i, 0))] * 2,
        out_shape=[jax.ShapeDtypeStruct((M, D), F32), jax.ShapeDtypeStruct((M, D), BF16)],
        compiler_params=_params(1, vmem),
        name="outproj",
    )(mixed, w_out, x, g2)


def _ffn_up_body(h_ref, wg_ref, wv_ref, cw_ref, cb_ref, o_ref, gs_ref, halo_ref, *, tm, tiles_per_batch):
    i = pl.program_id(0)
    j = pl.program_id(1)
    tn = o_ref.shape[1]
    h = h_ref[...]
    gs_ref[V7X_SUBLANES:V7X_SUBLANES + tm, :] = jnp.dot(h, wg_ref[...], preferred_element_type=F32)
    val = jnp.dot(h, wv_ref[...], preferred_element_type=F32)

    @pl.when(i % tiles_per_batch == 0)
    def _():
        gs_ref[0:V7X_SUBLANES, :] = jnp.zeros((V7X_SUBLANES, tn), F32)

    @pl.when(i % tiles_per_batch != 0)
    def _():
        gs_ref[0:V7X_SUBLANES, :] = halo_ref[j]

    gate = cb_ref[...] + cw_ref[CONV_FFN - 1:CONV_FFN, :] * gs_ref[V7X_SUBLANES:V7X_SUBLANES + tm, :]
    for t in range(CONV_FFN - 1):
        r0 = V7X_SUBLANES - (CONV_FFN - 1) + t
        gate = gate + cw_ref[t:t + 1, :] * gs_ref[r0:r0 + tm, :]
    halo_ref[j] = gs_ref[tm:tm + V7X_SUBLANES, :]
    o_ref[...] = (jax.nn.gelu(gate) * val).astype(o_ref.dtype)


def _ffn_up(h2, w_up, conv_w, conv_b, *, seq, tm, tn):
    M, D = h2.shape
    d_ff = w_up.shape[1] // 2
    nj = d_ff // tn
    vmem = 2 * (tm * D * 2 + 2 * D * tn * 2 + tm * tn * 2) + (tm + 8) * tn * 4 + nj * 8 * tn * 4 + 3 * tm * tn * 4
    return pl.pallas_call(
        functools.partial(_ffn_up_body, tm=tm, tiles_per_batch=seq // tm),
        grid=(M // tm, nj),
        in_specs=[pl.BlockSpec((tm, D), lambda i, j: (i, 0)),
                  pl.BlockSpec((D, tn), lambda i, j: (0, j)),
                  pl.BlockSpec((D, tn), lambda i, j: (0, j + nj)),
                  pl.BlockSpec((CONV_FFN, tn), lambda i, j: (0, j)),
                  pl.BlockSpec((1, tn), lambda i, j: (0, j))],
        out_specs=pl.BlockSpec((tm, tn), lambda i, j: (i, j)),
        out_shape=jax.ShapeDtypeStruct((M, d_ff), BF16),
        scratch_shapes=[pltpu.VMEM((tm + V7X_SUBLANES, tn), F32),
                        pltpu.VMEM((nj, V7X_SUBLANES, tn), F32)],
        compiler_params=_params(2, vmem),
        name="ffn_up",
    )(h2, w_up, w_up, conv_w, conv_b)


def _ffn_down_body(a_ref, w_ref, x_ref, g_ref, o_ref, *, final_norm):
    x2 = x_ref[...] + jnp.dot(a_ref[...], w_ref[...], preferred_element_type=F32)
    o_ref[...] = _rms(x2, g_ref[...]) if final_norm else x2


def _ffn_down(a, w_down, x1, g, *, tm, final_norm):
    M, d_ff = a.shape
    D = w_down.shape[1]
    vmem = 2 * (tm * d_ff * 2 + 2 * tm * D * 4) + d_ff * D * 2 + 2 * tm * D * 4
    return pl.pallas_call(
        functools.partial(_ffn_down_body, final_norm=final_norm),
        grid=(M // tm,),
        in_specs=[pl.BlockSpec((tm, d_ff), lambda i: (i, 0)),
                  pl.BlockSpec((d_ff, D), lambda i: (0, 0), pipeline_mode=pl.Buffered(1)),
                  pl.BlockSpec((tm, D), lambda i: (i, 0)),
                  pl.BlockSpec((1, D), lambda i: (0, 0))],
        out_specs=pl.BlockSpec((tm, D), lambda i: (i, 0)),
        out_shape=jax.ShapeDtypeStruct((M, D), F32),
        compiler_params=_params(1, vmem),
        name="ffn_down",
    )(a, w_down, x1, g)


def _layer(x2d, p, *, batch, seq):
    d_rnn = p["conv_w"].shape[1]
    att_w = len(ATT_GROUPS) * HEADS_PER_GROUP * HEAD_DIM
    off_q = 2 * d_rnn
    off_g = off_q + 3 * att_w
    w_in = p["w_in"].astype(BF16)

    h = _norm(x2d, p["norm1_g"][None, :], tm=512)
    xg = _matmul(h, w_in[:, :off_q], tm=1024, tn=768, act=None, name="inproj_rec")
    qkv = _matmul(h, w_in[:, off_q:off_g], tm=1024, tn=1024, act=None, name="inproj_qkv")
    gates = _matmul(h, w_in[:, off_g:], tm=1024, tn=1024, act="sigmoid", name="inproj_gates")

    band = _lru_band(d_rnn, d_rnn // LRU_BLOCKS)
    y_rec = _lru(xg, p["conv_w"], p["conv_b"][None, :], _band_tiles(p["lru_wr"], band), _band_tiles(p["lru_wi"], band),
                 p["lru_br"][None, :], p["lru_bi"][None, :], p["lru_lambda"][None, :], seq=seq, ts=256, band=band)

    slopes = _alibi_slopes(len(ATT_GROUPS) * HEADS_PER_GROUP).reshape(len(ATT_GROUPS), HEADS_PER_GROUP)
    att_parts = []
    for g, (window, dilation) in enumerate(ATT_GROUPS):
        att_parts += _attention_group(qkv, g, batch=batch, seq=seq, window=window, dilation=dilation,
                                      slopes=tuple(float(s) for s in slopes[g]))

    mixed = _mix(y_rec, att_parts, gates, p["w_rnn_out"].astype(BF16), p["w_att_out"].astype(BF16), tm=256, tn=1024)
    x1, h2 = _outproj(mixed, p["w_out"].astype(BF16), x2d, p["norm2_g"][None, :], tm=512)
    hmid = _ffn_up(h2, p["w_up"].astype(BF16), p["ffn_conv_w"], p["ffn_conv_b"][None, :], seq=seq, tm=1024, tn=512)
    return x1, hmid


def kernel(x, norm1_g, w_in, conv_w, conv_b, lru_wr, lru_br, lru_wi, lru_bi, lru_lambda, w_rnn_out, w_att_out, w_out,
           norm2_g, w_up, ffn_conv_w, ffn_conv_b, w_down, final_g):
    B, S, D = x.shape
    depth = w_in.shape[0]
    x2d = x.reshape(B * S, D)
    for l in range(depth):
        p = dict(norm1_g=norm1_g[l], w_in=w_in[l], conv_w=conv_w[l], conv_b=conv_b[l], lru_wr=lru_wr[l],
                 lru_br=lru_br[l], lru_wi=lru_wi[l], lru_bi=lru_bi[l], lru_lambda=lru_lambda[l],
                 w_rnn_out=w_rnn_out[l], w_att_out=w_att_out[l], w_out=w_out[l], norm2_g=norm2_g[l],
                 w_up=w_up[l], ffn_conv_w=ffn_conv_w[l], ffn_conv_b=ffn_conv_b[l])
        x1, hmid = _layer(x2d, p, batch=B, seq=S)
        x2d = _ffn_down(hmid, w_down[l].astype(BF16), x1, final_g[None, :], tm=256, final_norm=l == depth - 1)
    return x2d.reshape(B, S, D)
```

```python
import functools
import math

import numpy as np
import jax
import jax.numpy as jnp
from jax import lax
from jax.experimental import pallas as pl
from jax.experimental.pallas import tpu as pltpu

LRU_BLOCKS = 16
LRU_C = 8.0
CONV_REC = 4
ATT_GROUPS = ((128, 1), (512, 4), (2048, 16))
HEADS_PER_GROUP = 8
HEAD_DIM = 128
Q_BLOCK = 128
CONV_FFN = 3
EPS = 1e-6

V7X_LANES = 128
V7X_SUBLANES = 8
V7X_MXU_DIM = 256
V7X_VMEM_BYTES = 64 * 1024 * 1024

F32 = jnp.float32
BF16 = jnp.bfloat16


def _params(n_axes, vmem_bytes):
    limit = min(int(vmem_bytes * 1.25) + (4 << 20), V7X_VMEM_BYTES - (4 << 20))
    return pltpu.CompilerParams(dimension_semantics=("arbitrary",) * n_axes, vmem_limit_bytes=limit)


def _rms(xf, g):
    return xf * lax.rsqrt(jnp.mean(xf * xf, axis=-1, keepdims=True) + EPS) * g


def _norm_body(x_ref, g_ref, o_ref):
    o_ref[...] = _rms(x_ref[...], g_ref[...]).astype(o_ref.dtype)


def _norm(x, g, *, tm):
    M, D = x.shape
    return pl.pallas_call(
        _norm_body,
        grid=(M // tm,),
        in_specs=[pl.BlockSpec((tm, D), lambda i: (i, 0)), pl.BlockSpec((1, D), lambda i: (0, 0))],
        out_specs=pl.BlockSpec((tm, D), lambda i: (i, 0)),
        out_shape=jax.ShapeDtypeStruct((M, D), BF16),
        compiler_params=_params(1, 2 * tm * D * 6),
        name="norm1",
    )(x, g)


def _mm_body(a_ref, w_ref, o_ref, *, act, row_chunk):
    for c in range(a_ref.shape[0] // row_chunk):
        rows = slice(c * row_chunk, (c + 1) * row_chunk)
        acc = jnp.dot(a_ref[rows, :], w_ref[...], preferred_element_type=F32)
        if act == "sigmoid":
            acc = jax.nn.sigmoid(acc)
        o_ref[rows, :] = acc.astype(o_ref.dtype)


def _mm_subseq_body(a_ref, w_ref, o_ref, slab_ref, *, dilation, col_chunk):
    tm, tn = o_ref.shape
    chunk = dilation * Q_BLOCK
    for cc in range(tn // col_chunk):
        acc = jnp.dot(a_ref[...], w_ref[:, cc * col_chunk:(cc + 1) * col_chunk], preferred_element_type=F32)
        for s in range(col_chunk // V7X_LANES):
            slab = cc * (col_chunk // V7X_LANES) + s
            slab_ref[slab] = acc[:, s * V7X_LANES:(s + 1) * V7X_LANES]
            for ch in range(tm // chunk):
                for r in range(dilation):
                    piece = slab_ref[slab, pl.ds(ch * chunk + r, Q_BLOCK, stride=dilation), :]
                    row0 = ch * chunk + r * Q_BLOCK
                    o_ref[row0:row0 + Q_BLOCK, slab * V7X_LANES:(slab + 1) * V7X_LANES] = piece.astype(o_ref.dtype)


def _matmul(a, w, *, tm, tn, name, act=None, row_chunk=None, dilation=1):
    M, K = a.shape
    N = w.shape[1]
    assert M % tm == 0 and N % tn == 0
    vmem = 2 * (tm * K * 2 + K * tn * 2 + tm * tn * 2) + tm * tn * 4
    if dilation == 1:
        body = functools.partial(_mm_body, act=act, row_chunk=row_chunk or tm)
        scratch = []
    else:
        assert tm % (dilation * Q_BLOCK) == 0 and act is None
        body = functools.partial(_mm_subseq_body, dilation=dilation, col_chunk=V7X_MXU_DIM)
        scratch = [pltpu.VMEM((tn // V7X_LANES, tm, V7X_LANES), F32)]
        vmem += tm * tn * 4
    return pl.pallas_call(
        body,
        grid=(M // tm, N // tn),
        in_specs=[pl.BlockSpec((tm, K), lambda i, j: (i, 0)), pl.BlockSpec((K, tn), lambda i, j: (0, j))],
        out_specs=pl.BlockSpec((tm, tn), lambda i, j: (i, j)),
        out_shape=jax.ShapeDtypeStruct((M, N), BF16),
        scratch_shapes=scratch,
        compiler_params=_params(2, vmem),
        name=name,
    )(a, w)


def _lru_band(d_rnn, block):
    band = []
    for c0 in range(0, d_rnn, V7X_MXU_DIM):
        cn = min(V7X_MXU_DIM, d_rnn - c0)
        b0, b1 = c0 // block, (c0 + cn - 1) // block
        r0 = (block * b0) // V7X_LANES * V7X_LANES
        r1 = min(-(-(block * (b1 + 1)) // V7X_LANES) * V7X_LANES, d_rnn)
        band.append((c0, cn, r0, r1 - r0))
    return tuple(band)


def _band_tiles(w, band):
    dense = jax.scipy.linalg.block_diag(*[w[n] for n in range(w.shape[0])])
    kmax = max(k for _, _, _, k in band)
    tiles = []
    for c0, cn, r0, k in band:
        t = dense[r0:r0 + k, c0:c0 + cn]
        tiles.append(jnp.pad(t, ((0, kmax - k), (0, V7X_MXU_DIM - cn))))
    return jnp.stack(tiles).astype(BF16)


def _shift_rows(v, s, fill):
    T, C = v.shape
    if s % V7X_SUBLANES == 0:
        return jnp.concatenate([jnp.full((s, C), fill, v.dtype), v[:T - s]], axis=0)
    rolled = pltpu.roll(v, s, axis=0)
    row = lax.broadcasted_iota(jnp.int32, v.shape, 0)
    return jnp.where(row < s, jnp.asarray(fill, v.dtype), rolled)


def _lru_body(x_ref, g_ref, cw_ref, cb_ref, wr_ref, wi_ref, br_ref, bi_ref, lam_ref, y_ref,
              xs_ref, xf_ref, xb_ref, hc_ref, *, ts, tiles_per_batch, band, conv_chunk):
    t = pl.program_id(0)
    d_rnn = x_ref.shape[1]

    @pl.when(t % tiles_per_batch == 0)
    def _():
        xs_ref[0:V7X_SUBLANES, :] = jnp.zeros((V7X_SUBLANES, d_rnn), F32)
        hc_ref[...] = jnp.zeros(hc_ref.shape, F32)

    xs_ref[V7X_SUBLANES:V7X_SUBLANES + ts, :] = x_ref[...].astype(F32)
    for c in range(0, d_rnn, conv_chunk):
        cs = slice(c, c + conv_chunk)
        acc = cb_ref[:, cs] + cw_ref[CONV_REC - 1:CONV_REC, cs] * xs_ref[V7X_SUBLANES:V7X_SUBLANES + ts, cs]
        for i in range(CONV_REC - 1):
            r0 = V7X_SUBLANES - (CONV_REC - 1) + i
            acc = acc + cw_ref[i:i + 1, cs] * xs_ref[r0:r0 + ts, cs]
        xf_ref[:, cs] = acc
        xb_ref[:, cs] = acc.astype(BF16)
    xs_ref[0:V7X_SUBLANES, :] = xs_ref[ts:ts + V7X_SUBLANES, :]

    for j, (c0, cn, r0, k) in enumerate(band):
        cs = slice(c0, c0 + cn)
        win = xb_ref[:, r0:r0 + k]
        zr = jnp.dot(win, wr_ref[j, :k, :], preferred_element_type=F32)[:, :cn] + br_ref[:, cs]
        zi = jnp.dot(win, wi_ref[j, :k, :], preferred_element_type=F32)[:, :cn] + bi_ref[:, cs]
        r = jax.nn.sigmoid(zr)
        gi = jax.nn.sigmoid(zi)
        log_a = -LRU_C * r * jax.nn.softplus(-lam_ref[:, cs])
        a = jnp.exp(log_a)
        b = jnp.sqrt(1.0 - jnp.exp(2.0 * log_a)) * (gi * xf_ref[:, cs])
        s = 1
        while s < ts:
            b = a * _shift_rows(b, s, 0.0) + b
            a = a * _shift_rows(a, s, 1.0)
            s *= 2
        h = a * hc_ref[0:1, cs] + b
        hc_ref[0:1, cs] = h[ts - 1:ts, :]
        y_ref[:, cs] = (h * jax.nn.gelu(g_ref[:, cs].astype(F32))).astype(y_ref.dtype)


def _lru(xg, conv_w, conv_b, wr_t, wi_t, br, bi, lam, *, seq, ts, band):
    M = xg.shape[0]
    d_rnn = xg.shape[1] // 2
    nt, kmax, _ = wr_t.shape
    row = lambda i: (0, 0)
    vmem = (2 * 3 * ts * d_rnn * 2 + 2 * 2 * nt * kmax * V7X_MXU_DIM * 2
            + (ts + V7X_SUBLANES) * d_rnn * 4 + ts * d_rnn * 6 + 16 * ts * V7X_MXU_DIM * 4)
    return pl.pallas_call(
        functools.partial(_lru_body, ts=ts, tiles_per_batch=seq // ts, band=band, conv_chunk=3 * V7X_LANES),
        grid=(M // ts,),
        in_specs=[
            pl.BlockSpec((ts, d_rnn), lambda i: (i, 0)),
            pl.BlockSpec((ts, d_rnn), lambda i: (i, 1)),
            pl.BlockSpec((CONV_REC, d_rnn), row),
            pl.BlockSpec((1, d_rnn), row),
            pl.BlockSpec((nt, kmax, V7X_MXU_DIM), lambda i: (0, 0, 0)),
            pl.BlockSpec((nt, kmax, V7X_MXU_DIM), lambda i: (0, 0, 0)),
            pl.BlockSpec((1, d_rnn), row),
            pl.BlockSpec((1, d_rnn), row),
            pl.BlockSpec((1, d_rnn), row),
        ],
        out_specs=pl.BlockSpec((ts, d_rnn), lambda i: (i, 0)),
        out_shape=jax.ShapeDtypeStruct((M, d_rnn), BF16),
        scratch_shapes=[
            pltpu.VMEM((ts + V7X_SUBLANES, d_rnn), F32),
            pltpu.VMEM((ts, d_rnn), F32),
            pltpu.VMEM((ts, d_rnn), BF16),
            pltpu.VMEM((V7X_SUBLANES, d_rnn), F32),
        ],
        compiler_params=_params(1, vmem),
        name="rglru",
    )(xg, xg, conv_w, conv_b, wr_t, wi_t, br, bi, lam)


def _alibi_slopes(n):
    def pow2_slopes(m):
        start = 2.0 ** (-8.0 / m)
        return [start ** (i + 1) for i in range(m)]
    c = 2 ** int(math.floor(math.log2(n)))
    s = pow2_slopes(c) + pow2_slopes(2 * c)[0::2][: n - c]
    return np.sort(np.asarray(s, np.float32))[::-1].copy()


def _attn_body(q_ref, kc_ref, kp_ref, vc_ref, vp_ref, o_ref, l_ref, *, dilation, span, slopes):
    n = pl.program_id(1)
    r = pl.program_id(2)
    qi = lax.broadcasted_iota(jnp.int32, (Q_BLOCK, 2 * Q_BLOCK), 0)
    ki = lax.broadcasted_iota(jnp.int32, (Q_BLOCK, 2 * Q_BLOCK), 1)
    steps = qi + Q_BLOCK - ki
    first_key = jnp.where(n > 0, 0, Q_BLOCK)
    valid = (steps >= 0) & (steps <= span) & (ki >= first_key)
    dist = (steps * dilation).astype(F32)
    scale = HEAD_DIM ** -0.5
    out_rows = slice(None) if dilation == 1 else pl.ds(r, Q_BLOCK, stride=dilation)
    for h in range(HEADS_PER_GROUP):
        hs = slice(h * HEAD_DIM, (h + 1) * HEAD_DIM)
        kcat = jnp.concatenate([kp_ref[:, hs], kc_ref[:, hs]], axis=0)
        vcat = jnp.concatenate([vp_ref[:, hs], vc_ref[:, hs]], axis=0)
        s = lax.dot_general(q_ref[:, hs], kcat, (((1,), (1,)), ((), ())), preferred_element_type=F32) * scale
        s = jnp.where(valid, s - float(slopes[h]) * dist, -jnp.inf)
        m = jnp.max(s, axis=-1, keepdims=True)
        p = jnp.exp(s - m)
        den = jnp.sum(p, axis=-1, keepdims=True)
        o = jnp.dot(p.astype(BF16), vcat, preferred_element_type=F32) / den
        o_ref[h, out_rows, :] = o
        l_ref[h, out_rows, :] = jnp.broadcast_to(m + jnp.log(den), (Q_BLOCK, HEAD_DIM))


def _attention_group(qkv, *, batch, seq, window, dilation, slopes):
    M, width = qkv.shape
    gw = HEADS_PER_GROUP * HEAD_DIM
    assert width == 3 * gw and seq % (dilation * Q_BLOCK) == 0
    nb = seq // (dilation * Q_BLOCK)
    blocks_per_batch = nb * dilation

    def cur(sec):
        return lambda b, n, r: (b * blocks_per_batch + n * dilation + r, sec)

    def prev(sec):
        return lambda b, n, r: (b * blocks_per_batch + jnp.maximum(n - 1, 0) * dilation + r, sec)

    blk = (Q_BLOCK, gw)
    chunk = dilation * Q_BLOCK
    out_blk = (HEADS_PER_GROUP, chunk, HEAD_DIM)
    out_shape = jax.ShapeDtypeStruct((HEADS_PER_GROUP, M, HEAD_DIM), F32)
    return pl.pallas_call(
        functools.partial(_attn_body, dilation=dilation, span=window // dilation, slopes=slopes),
        grid=(batch, nb, dilation),
        in_specs=[pl.BlockSpec(blk, cur(0)), pl.BlockSpec(blk, cur(1)), pl.BlockSpec(blk, prev(1)),
                  pl.BlockSpec(blk, cur(2)), pl.BlockSpec(blk, prev(2))],
        out_specs=[pl.BlockSpec(out_blk, lambda b, n, r: (0, b * nb + n, 0))] * 2,
        out_shape=[out_shape, out_shape],
        compiler_params=_params(3, 2 * (5 * Q_BLOCK * gw * 2 + 2 * chunk * gw * 4) + (4 << 20)),
        name=f"attn_d{dilation}",
    )(qkv, qkv, qkv, qkv, qkv)


def _mix_body(y_ref, o1_ref, l1_ref, o2_ref, l2_ref, o3_ref, l3_ref, gr_ref, ga_ref, wr_ref, wa_ref,
              out_ref, *, row_chunk):
    for c in range(y_ref.shape[0] // row_chunk):
        rows = slice(c * row_chunk, (c + 1) * row_chunk)
        heads = []
        for h in range(HEADS_PER_GROUP):
            l1, l2, l3 = l1_ref[h, rows, :], l2_ref[h, rows, :], l3_ref[h, rows, :]
            m = jnp.maximum(jnp.maximum(l1, l2), l3)
            e1, e2, e3 = jnp.exp(l1 - m), jnp.exp(l2 - m), jnp.exp(l3 - m)
            den = e1 + e2 + e3
            att = (e1 / den) * o1_ref[h, rows, :] + (e2 / den) * o2_ref[h, rows, :] + (e3 / den) * o3_ref[h, rows, :]
            heads.append(att.astype(BF16))
        att = jnp.concatenate(heads, axis=1)
        ya = jnp.dot(y_ref[rows, :], wr_ref[...], preferred_element_type=F32)
        yb = jnp.dot(att, wa_ref[...], preferred_element_type=F32)
        out_ref[rows, :] = (gr_ref[rows, :].astype(F32) * ya + ga_ref[rows, :].astype(F32) * yb).astype(out_ref.dtype)


def _mix(y_rec, att_parts, gates, w_rnn, w_att, *, tm, row_chunk):
    M, d_rnn = y_rec.shape
    aw, D = w_att.shape
    rowblk = lambda w: pl.BlockSpec((tm, w), lambda i: (i, 0))
    headblk = pl.BlockSpec((HEADS_PER_GROUP, tm, HEAD_DIM), lambda i: (0, i, 0))
    resident = lambda r: pl.BlockSpec((r, D), lambda i: (0, 0), pipeline_mode=pl.Buffered(1))
    vmem = (2 * (tm * d_rnn * 2 + 6 * tm * aw * 4 + 2 * tm * D * 2 + tm * D * 2) + (d_rnn + aw) * D * 2
            + 4 * row_chunk * D * 4)
    return pl.pallas_call(
        functools.partial(_mix_body, row_chunk=row_chunk),
        grid=(M // tm,),
        in_specs=[rowblk(d_rnn)] + [headblk] * 6 + [
            pl.BlockSpec((tm, D), lambda i: (i, 0)),
            pl.BlockSpec((tm, D), lambda i: (i, 1)),
            resident(d_rnn), resident(aw)],
        out_specs=rowblk(D),
        out_shape=jax.ShapeDtypeStruct((M, D), BF16),
        compiler_params=_params(1, vmem),
        name="mix",
    )(y_rec, *att_parts, gates, gates, w_rnn, w_att)


def _outproj_body(m_ref, w_ref, x_ref, g_ref, x1_ref, h2_ref, *, row_chunk):
    for c in range(m_ref.shape[0] // row_chunk):
        rows = slice(c * row_chunk, (c + 1) * row_chunk)
        x1 = x_ref[rows, :] + jnp.dot(m_ref[rows, :], w_ref[...], preferred_element_type=F32)
        x1_ref[rows, :] = x1
        h2_ref[rows, :] = _rms(x1, g_ref[...]).astype(h2_ref.dtype)


def _outproj(mixed, w_out, x, g2, *, tm, row_chunk):
    M, D = x.shape
    vmem = 2 * (tm * D * 2 + D * D * 2 + tm * D * 4 + tm * D * 4 + tm * D * 2) + 2 * tm * D * 4
    return pl.pallas_call(
        functools.partial(_outproj_body, row_chunk=row_chunk),
        grid=(M // tm,),
        in_specs=[pl.BlockSpec((tm, D), lambda i: (i, 0)), pl.BlockSpec((D, D), lambda i: (0, 0)),
                  pl.BlockSpec((tm, D), lambda i: (i, 0)), pl.BlockSpec((1, D), lambda i: (0, 0))],
        out_specs=[pl.BlockSpec((tm, D), lambda i: (i, 0))] * 2,
        out_shape=[jax.ShapeDtypeStruct((M, D), F32), jax.ShapeDtypeStruct((M, D), BF16)],
        compiler_params=_params(1, vmem),
        name="outproj",
    )(mixed, w_out, x, g2)


def _ffn_up_body(h_ref, wg_ref, wv_ref, cw_ref, cb_ref, o_ref, gs_ref, halo_ref, *, tiles_per_batch, row_chunk):
    i = pl.program_id(0)
    j = pl.program_id(1)
    tm, tn = o_ref.shape

    @pl.when(i % tiles_per_batch == 0)
    def _():
        gs_ref[0:V7X_SUBLANES, :] = jnp.zeros((V7X_SUBLANES, tn), F32)

    @pl.when(i % tiles_per_batch != 0)
    def _():
        gs_ref[0:V7X_SUBLANES, :] = halo_ref[j]

    for c in range(tm // row_chunk):
        rows = slice(c * row_chunk, (c + 1) * row_chunk)
        base = V7X_SUBLANES + c * row_chunk
        h = h_ref[rows, :]
        gs_ref[base:base + row_chunk, :] = jnp.dot(h, wg_ref[...], preferred_element_type=F32)
        val = jnp.dot(h, wv_ref[...], preferred_element_type=F32)
        gate = cb_ref[...] + cw_ref[CONV_FFN - 1:CONV_FFN, :] * gs_ref[base:base + row_chunk, :]
        for t in range(CONV_FFN - 1):
            r0 = base - (CONV_FFN - 1) + t
            gate = gate + cw_ref[t:t + 1, :] * gs_ref[r0:r0 + row_chunk, :]
        o_ref[rows, :] = (jax.nn.gelu(gate) * val).astype(o_ref.dtype)
    halo_ref[j] = gs_ref[tm:tm + V7X_SUBLANES, :]


def _ffn_up(h2, w_up, conv_w, conv_b, *, seq, tm, tn, row_chunk):
    M, D = h2.shape
    d_ff = w_up.shape[1] // 2
    nj = d_ff // tn
    vmem = (2 * (tm * D * 2 + 2 * D * tn * 2 + tm * tn * 2) + (tm + 8) * tn * 4 + nj * 8 * tn * 4
            + 6 * row_chunk * tn * 4)
    return pl.pallas_call(
        functools.partial(_ffn_up_body, tiles_per_batch=seq // tm, row_chunk=row_chunk),
        grid=(M // tm, nj),
        in_specs=[pl.BlockSpec((tm, D), lambda i, j: (i, 0)),
                  pl.BlockSpec((D, tn), lambda i, j: (0, j)),
                  pl.BlockSpec((D, tn), lambda i, j: (0, j + nj)),
                  pl.BlockSpec((CONV_FFN, tn), lambda i, j: (0, j)),
                  pl.BlockSpec((1, tn), lambda i, j: (0, j))],
        out_specs=pl.BlockSpec((tm, tn), lambda i, j: (i, j)),
        out_shape=jax.ShapeDtypeStruct((M, d_ff), BF16),
        scratch_shapes=[pltpu.VMEM((tm + V7X_SUBLANES, tn), F32),
                        pltpu.VMEM((nj, V7X_SUBLANES, tn), F32)],
        compiler_params=_params(2, vmem),
        name="ffn_up",
    )(h2, w_up, w_up, conv_w, conv_b)


def _ffn_down_body(a_ref, w_ref, x_ref, g_ref, o_ref, *, final_norm, row_chunk):
    for c in range(a_ref.shape[0] // row_chunk):
        rows = slice(c * row_chunk, (c + 1) * row_chunk)
        x2 = x_ref[rows, :] + jnp.dot(a_ref[rows, :], w_ref[...], preferred_element_type=F32)
        o_ref[rows, :] = _rms(x2, g_ref[...]) if final_norm else x2


def _ffn_down(a, w_down, x1, g, *, tm, row_chunk, final_norm):
    M, d_ff = a.shape
    D = w_down.shape[1]
    vmem = 2 * (tm * d_ff * 2 + 2 * tm * D * 4) + d_ff * D * 2 + 2 * tm * D * 4
    return pl.pallas_call(
        functools.partial(_ffn_down_body, final_norm=final_norm, row_chunk=row_chunk),
        grid=(M // tm,),
        in_specs=[pl.BlockSpec((tm, d_ff), lambda i: (i, 0)),
                  pl.BlockSpec((d_ff, D), lambda i: (0, 0), pipeline_mode=pl.Buffered(1)),
                  pl.BlockSpec((tm, D), lambda i: (i, 0)),
                  pl.BlockSpec((1, D), lambda i: (0, 0))],
        out_specs=pl.BlockSpec((tm, D), lambda i: (i, 0)),
        out_shape=jax.ShapeDtypeStruct((M, D), F32),
        compiler_params=_params(1, vmem),
        name="ffn_down",
    )(a, w_down, x1, g)


def _layer(x2d, p, *, batch, seq):
    d_rnn = p["conv_w"].shape[1]
    gw = HEADS_PER_GROUP * HEAD_DIM
    att_w = len(ATT_GROUPS) * gw
    off_q = 2 * d_rnn
    off_g = off_q + 3 * att_w
    w_in = p["w_in"].astype(BF16)

    h = _norm(x2d, p["norm1_g"][None, :], tm=512)
    xg = _matmul(h, w_in[:, :off_q], tm=2048, tn=768, name="inproj_rec")
    gates = _matmul(h, w_in[:, off_g:], tm=2048, tn=1024, act="sigmoid", row_chunk=512, name="inproj_gates")

    band = _lru_band(d_rnn, d_rnn // LRU_BLOCKS)
    y_rec = _lru(xg, p["conv_w"], p["conv_b"][None, :], _band_tiles(p["lru_wr"], band), _band_tiles(p["lru_wi"], band),
                 p["lru_br"][None, :], p["lru_bi"][None, :], p["lru_lambda"][None, :], seq=seq, ts=256, band=band)

    slopes = _alibi_slopes(len(ATT_GROUPS) * HEADS_PER_GROUP).reshape(len(ATT_GROUPS), HEADS_PER_GROUP)
    att_parts = []
    for g, (window, dilation) in enumerate(ATT_GROUPS):
        w_g = jnp.concatenate([w_in[:, off_q + sec * att_w + g * gw: off_q + sec * att_w + (g + 1) * gw]
                               for sec in range(3)], axis=1)
        qkv = _matmul(h, w_g, tm=2048, tn=1024, dilation=dilation, name=f"inproj_qkv_d{dilation}")
        att_parts += _attention_group(qkv, batch=batch, seq=seq, window=window, dilation=dilation,
                                      slopes=tuple(float(s) for s in slopes[g]))

    mixed = _mix(y_rec, att_parts, gates, p["w_rnn_out"].astype(BF16), p["w_att_out"].astype(BF16),
                 tm=256, row_chunk=128)
    x1, h2 = _outproj(mixed, p["w_out"].astype(BF16), x2d, p["norm2_g"][None, :], tm=512, row_chunk=256)
    hmid = _ffn_up(h2, p["w_up"].astype(BF16), p["ffn_conv_w"], p["ffn_conv_b"][None, :], seq=seq,
                   tm=2048, tn=512, row_chunk=256)
    return x1, hmid


def kernel(x, norm1_g, w_in, conv_w, conv_b, lru_wr, lru_br, lru_wi, lru_bi, lru_lambda, w_rnn_out, w_att_out, w_out,
           norm2_g, w_up, ffn_conv_w, ffn_conv_b, w_down, final_g):
    B, S, D = x.shape
    depth = w_in.shape[0]
    x2d = x.reshape(B * S, D)
    for l in range(depth):
        p = dict(norm1_g=norm1_g[l], w_in=w_in[l], conv_w=conv_w[l], conv_b=conv_b[l], lru_wr=lru_wr[l],
                 lru_br=lru_br[l], lru_wi=lru_wi[l], lru_bi=lru_bi[l], lru_lambda=lru_lambda[l],
                 w_rnn_out=w_rnn_out[l], w_att_out=w_att_out[l], w_out=w_out[l], norm2_g=norm2_g[l],
                 w_up=w_up[l], ffn_conv_w=ffn_conv_w[l], ffn_conv_b=ffn_conv_b[l])
        x1, hmid = _layer(x2d, p, batch=B, seq=S)
        x2d = _ffn_down(hmid, w_down[l].astype(BF16), x1, final_g[None, :], tm=256, row_chunk=128,
                        final_norm=l == depth - 1)
    return x2d.reshape(B, S, D)
```

```python
import functools
import math

import numpy as np
import jax
import jax.numpy as jnp
from jax import lax
from jax.experimental import pallas as pl
from jax.experimental.pallas import tpu as pltpu

LRU_BLOCKS = 16
LRU_C = 8.0
CONV_REC = 4
ATT_GROUPS = ((128, 1), (512, 4), (2048, 16))
HEADS_PER_GROUP = 8
HEAD_DIM = 128
Q_BLOCK = 128
CONV_FFN = 3
EPS = 1e-6

V7X_LANES = 128
V7X_SUBLANES = 8
V7X_MXU_DIM = 256
V7X_VMEM_BYTES = 64 * 1024 * 1024

F32 = jnp.float32
BF16 = jnp.bfloat16


def _params(n_axes, vmem_bytes):
    limit = min(int(vmem_bytes * 1.25) + (4 << 20), V7X_VMEM_BYTES - (4 << 20))
    return pltpu.CompilerParams(dimension_semantics=("arbitrary",) * n_axes, vmem_limit_bytes=limit)


def _rms(xf, g):
    return xf * lax.rsqrt(jnp.mean(xf * xf, axis=-1, keepdims=True) + EPS) * g


def _norm_body(x_ref, g_ref, o_ref):
    o_ref[...] = _rms(x_ref[...], g_ref[...]).astype(o_ref.dtype)


def _norm(x, g, *, tm):
    M, D = x.shape
    return pl.pallas_call(
        _norm_body,
        grid=(M // tm,),
        in_specs=[pl.BlockSpec((tm, D), lambda i: (i, 0)), pl.BlockSpec((1, D), lambda i: (0, 0))],
        out_specs=pl.BlockSpec((tm, D), lambda i: (i, 0)),
        out_shape=jax.ShapeDtypeStruct((M, D), BF16),
        compiler_params=_params(1, 2 * tm * D * 6),
        name="norm1",
    )(x, g)


def _mm_body(a_ref, w_ref, o_ref, *, act, row_chunk):
    for c in range(a_ref.shape[0] // row_chunk):
        rows = slice(c * row_chunk, (c + 1) * row_chunk)
        acc = jnp.dot(a_ref[rows, :], w_ref[...], preferred_element_type=F32)
        if act == "sigmoid":
            acc = jax.nn.sigmoid(acc)
        o_ref[rows, :] = acc.astype(o_ref.dtype)


BF16_TILE_ROWS = 16


def _subseq_perm(dilation):
    lb = BF16_TILE_ROWS * dilation
    assert V7X_MXU_DIM % lb == 0
    p = np.arange(V7X_MXU_DIM)
    q = p % lb
    perm = np.zeros((V7X_MXU_DIM, V7X_MXU_DIM), np.float32)
    perm[p, p // lb * lb + (q % BF16_TILE_ROWS) * dilation + q // BF16_TILE_ROWS] = 1.0
    return perm


def _mm_subseq_body(a_ref, p_ref, w_ref, o_ref, asub_ref, *, dilation):
    tm = a_ref.shape[0]
    lb = BF16_TILE_ROWS * dilation
    chunk = dilation * Q_BLOCK

    @pl.when(pl.program_id(1) == 0)
    def _():
        for blk in range(tm // V7X_MXU_DIM):
            t = jnp.dot(p_ref[...], a_ref[blk * V7X_MXU_DIM:(blk + 1) * V7X_MXU_DIM, :],
                        preferred_element_type=F32).astype(BF16)
            for l in range(V7X_MXU_DIM // lb):
                g = blk * (V7X_MXU_DIM // lb) + l
                base = (g // V7X_SUBLANES) * chunk + (g % V7X_SUBLANES) * BF16_TILE_ROWS
                for r in range(dilation):
                    src = l * lb + r * BF16_TILE_ROWS
                    dst = base + r * Q_BLOCK
                    asub_ref[dst:dst + BF16_TILE_ROWS, :] = t[src:src + BF16_TILE_ROWS, :]

    o_ref[...] = jnp.dot(asub_ref[...], w_ref[...], preferred_element_type=F32).astype(o_ref.dtype)


def _matmul(a, w, *, tm, tn, name, act=None, row_chunk=None, dilation=1):
    M, K = a.shape
    N = w.shape[1]
    assert M % tm == 0 and N % tn == 0
    vmem = 2 * (tm * K * 2 + K * tn * 2 + tm * tn * 2) + tm * tn * 4
    a_spec = pl.BlockSpec((tm, K), lambda i, j: (i, 0))
    w_spec = pl.BlockSpec((K, tn), lambda i, j: (0, j))
    if dilation == 1:
        body = functools.partial(_mm_body, act=act, row_chunk=row_chunk or tm)
        in_specs, operands, scratch = [a_spec, w_spec], (a, w), []
    else:
        assert tm % (dilation * Q_BLOCK) == 0 and act is None
        body = functools.partial(_mm_subseq_body, dilation=dilation)
        perm = jnp.asarray(_subseq_perm(dilation), BF16)
        in_specs = [a_spec, pl.BlockSpec(perm.shape, lambda i, j: (0, 0)), w_spec]
        operands = (a, perm, w)
        scratch = [pltpu.VMEM((tm, K), BF16)]
        vmem += tm * K * 2
    return pl.pallas_call(
        body,
        grid=(M // tm, N // tn),
        in_specs=in_specs,
        out_specs=pl.BlockSpec((tm, tn), lambda i, j: (i, j)),
        out_shape=jax.ShapeDtypeStruct((M, N), BF16),
        scratch_shapes=scratch,
        compiler_params=_params(2, vmem),
        name=name,
    )(*operands)


def _lru_band(d_rnn, block):
    band = []
    for c0 in range(0, d_rnn, V7X_MXU_DIM):
        cn = min(V7X_MXU_DIM, d_rnn - c0)
        b0, b1 = c0 // block, (c0 + cn - 1) // block
        r0 = (block * b0) // V7X_LANES * V7X_LANES
        r1 = min(-(-(block * (b1 + 1)) // V7X_LANES) * V7X_LANES, d_rnn)
        band.append((c0, cn, r0, r1 - r0))
    return tuple(band)


def _band_tiles(w, band):
    dense = jax.scipy.linalg.block_diag(*[w[n] for n in range(w.shape[0])])
    kmax = max(k for _, _, _, k in band)
    tiles = []
    for c0, cn, r0, k in band:
        t = dense[r0:r0 + k, c0:c0 + cn]
        tiles.append(jnp.pad(t, ((0, kmax - k), (0, V7X_MXU_DIM - cn))))
    return jnp.stack(tiles).astype(BF16)


def _segment_order(ts):
    p = np.arange(ts)
    perm = np.zeros((ts, ts), np.float32)
    perm[p, (p % V7X_SUBLANES) * (ts // V7X_SUBLANES) + p // V7X_SUBLANES] = 1.0
    return perm


def _shift_sublanes(v, s, fill):
    row = lax.broadcasted_iota(jnp.int32, v.shape, 0)
    return jnp.where(row < s, fill, pltpu.roll(v, s, axis=0))


_LRU_HALO = (CONV_REC - 1) * V7X_SUBLANES


def _lru_body(h_ref, pin_ref, pout_ref, w_ref, cw_ref, cb_ref, wr_ref, wi_ref, br_ref, bi_ref, lam_ref, y_ref,
              hil_ref, xs_ref, xf_ref, xb_ref, gs_ref, tail_ref, hc_ref, *, ts, tiles_per_batch, band):
    d_rnn = y_ref.shape[1]
    nseg = V7X_SUBLANES
    seg_len = ts // nseg

    @pl.when(pl.program_id(0) % tiles_per_batch == 0)
    def _():
        tail_ref[...] = jnp.zeros(tail_ref.shape, F32)
        hc_ref[...] = jnp.zeros(hc_ref.shape, F32)

    hil_ref[...] = jnp.dot(pin_ref[...], h_ref[...], preferred_element_type=F32).astype(BF16)

    row8 = lax.broadcasted_iota(jnp.int32, (V7X_SUBLANES, V7X_MXU_DIM), 0)

    def project(c0, cn):
        cs = slice(c0, c0 + cn)
        xs_ref[_LRU_HALO:_LRU_HALO + ts, cs] = jnp.dot(hil_ref[...], w_ref[:, cs], preferred_element_type=F32)
        gs_ref[:, cs] = jnp.dot(hil_ref[...], w_ref[:, d_rnn + c0:d_rnn + c0 + cn], preferred_element_type=F32)
        for m in range(CONV_REC - 1):
            r0 = ts + m * V7X_SUBLANES
            cur8 = pltpu.roll(xs_ref[r0:r0 + V7X_SUBLANES, cs], 1, axis=0)
            prev8 = pltpu.roll(tail_ref[m * V7X_SUBLANES:(m + 1) * V7X_SUBLANES, cs], 1, axis=0)
            xs_ref[m * V7X_SUBLANES:(m + 1) * V7X_SUBLANES, cs] = jnp.where(row8[:, :cn] == 0, prev8, cur8)
        tail_ref[:, cs] = xs_ref[ts:ts + _LRU_HALO, cs]
        acc = cb_ref[:, cs] + cw_ref[CONV_REC - 1:CONV_REC, cs] * xs_ref[_LRU_HALO:_LRU_HALO + ts, cs]
        for i in range(CONV_REC - 1):
            r0 = i * V7X_SUBLANES
            acc = acc + cw_ref[i:i + 1, cs] * xs_ref[r0:r0 + ts, cs]
        xf_ref[:, cs] = acc
        xb_ref[:, cs] = acc.astype(BF16)

    def emit(cols, y_il):
        y_ref[:, cols] = jnp.dot(pout_ref[...], y_il, preferred_element_type=F32).astype(y_ref.dtype)

    projected = 0
    pending = None
    for j, (c0, cn, r0, k) in enumerate(band):
        ahead = band[min(j + 1, len(band) - 1)]
        while projected < len(band) and band[projected][0] < ahead[2] + ahead[3]:
            project(*band[projected][:2])
            projected += 1
        cs = slice(c0, c0 + cn)
        win = xb_ref[:, r0:r0 + k]
        zr = jnp.dot(win, wr_ref[j, :k, :], preferred_element_type=F32)[:, :cn] + br_ref[:, cs]
        zi = jnp.dot(win, wi_ref[j, :k, :], preferred_element_type=F32)[:, :cn] + bi_ref[:, cs]
        if pending is not None:
            emit(*pending)
        r = jax.nn.sigmoid(zr)
        gi = jax.nn.sigmoid(zi)
        log_a = -LRU_C * r * jax.nn.softplus(-lam_ref[:, cs])
        a = jnp.exp(log_a)
        b = jnp.sqrt(1.0 - jnp.exp(2.0 * log_a)) * (gi * xf_ref[:, cs])
        v = lambda x, i: x[i * V7X_SUBLANES:(i + 1) * V7X_SUBLANES]
        h_loc, a_cum = [v(b, 0)], [v(a, 0)]
        for i in range(1, seg_len):
            h_loc.append(v(a, i) * h_loc[-1] + v(b, i))
            a_cum.append(v(a, i) * a_cum[-1])
        a_inc, h_inc = a_cum[-1], h_loc[-1]
        s = 1
        while s < nseg:
            h_inc = a_inc * _shift_sublanes(h_inc, s, 0.0) + h_inc
            a_inc = a_inc * _shift_sublanes(a_inc, s, 1.0)
            s *= 2
        c_in = hc_ref[0:1, cs]
        c_end = a_inc * c_in + h_inc
        c_seg = _shift_sublanes(c_end, 1, c_in)
        hc_ref[0:1, cs] = c_end[nseg - 1:nseg, :]
        h = jnp.concatenate([a_cum[i] * c_seg + h_loc[i] for i in range(seg_len)], axis=0)
        pending = (cs, (h * jax.nn.gelu(gs_ref[:, cs])).astype(BF16))
    emit(*pending)


def _lru(h, w_rec, conv_w, conv_b, wr_t, wi_t, br, bi, lam, *, seq, ts, band):
    M, D = h.shape
    d_rnn = w_rec.shape[1] // 2
    nt, kmax, _ = wr_t.shape
    perm = _segment_order(ts)
    row = lambda i: (0, 0)
    resident = lambda shape: pl.BlockSpec(shape, lambda i: (0,) * len(shape), pipeline_mode=pl.Buffered(1))
    vmem = (D * 2 * d_rnn * 2 + 2 * nt * kmax * V7X_MXU_DIM * 2 + 2 * (ts * D * 2 + ts * d_rnn * 2) + ts * D * 2
            + (ts + _LRU_HALO) * d_rnn * 4 + ts * d_rnn * 10 + 24 * ts * V7X_MXU_DIM * 4)
    return pl.pallas_call(
        functools.partial(_lru_body, ts=ts, tiles_per_batch=seq // ts, band=band),
        grid=(M // ts,),
        in_specs=[
            pl.BlockSpec((ts, D), lambda i: (i, 0)),
            pl.BlockSpec((ts, ts), row),
            pl.BlockSpec((ts, ts), row),
            resident((D, 2 * d_rnn)),
            pl.BlockSpec((CONV_REC, d_rnn), row),
            pl.BlockSpec((1, d_rnn), row),
            resident((nt, kmax, V7X_MXU_DIM)),
            resident((nt, kmax, V7X_MXU_DIM)),
            pl.BlockSpec((1, d_rnn), row),
            pl.BlockSpec((1, d_rnn), row),
            pl.BlockSpec((1, d_rnn), row),
        ],
        out_specs=pl.BlockSpec((ts, d_rnn), lambda i: (i, 0)),
        out_shape=jax.ShapeDtypeStruct((M, d_rnn), BF16),
        scratch_shapes=[
            pltpu.VMEM((ts, D), BF16),
            pltpu.VMEM((ts + _LRU_HALO, d_rnn), F32),
            pltpu.VMEM((ts, d_rnn), F32),
            pltpu.VMEM((ts, d_rnn), BF16),
            pltpu.VMEM((ts, d_rnn), F32),
            pltpu.VMEM((_LRU_HALO, d_rnn), F32),
            pltpu.VMEM((V7X_SUBLANES, d_rnn), F32),
        ],
        compiler_params=_params(1, vmem),
        name="rglru",
    )(h, jnp.asarray(perm, BF16), jnp.asarray(perm.T, BF16), w_rec, conv_w, conv_b, wr_t, wi_t, br, bi, lam)


def _alibi_slopes(n):
    def pow2_slopes(m):
        start = 2.0 ** (-8.0 / m)
        return [start ** (i + 1) for i in range(m)]
    c = 2 ** int(math.floor(math.log2(n)))
    s = pow2_slopes(c) + pow2_slopes(2 * c)[0::2][: n - c]
    return np.sort(np.asarray(s, np.float32))[::-1].copy()


def _attn_body(q_ref, kc_ref, kp_ref, vc_ref, vp_ref, o_ref, l_ref, *, dilation, span, slopes):
    n = pl.program_id(1)
    r = pl.program_id(2)
    qi = lax.broadcasted_iota(jnp.int32, (Q_BLOCK, 2 * Q_BLOCK), 0)
    ki = lax.broadcasted_iota(jnp.int32, (Q_BLOCK, 2 * Q_BLOCK), 1)
    steps = qi + Q_BLOCK - ki
    first_key = jnp.where(n > 0, 0, Q_BLOCK)
    valid = (steps >= 0) & (steps <= span) & (ki >= first_key)
    dist = (steps * dilation).astype(F32)
    scale = HEAD_DIM ** -0.5
    out_rows = slice(None) if dilation == 1 else pl.ds(r, Q_BLOCK, stride=dilation)
    lane = lax.broadcasted_iota(jnp.int32, (Q_BLOCK, HEAD_DIM), 1)
    lse = jnp.zeros((Q_BLOCK, HEAD_DIM), F32)
    for h in range(HEADS_PER_GROUP):
        hs = slice(h * HEAD_DIM, (h + 1) * HEAD_DIM)
        kcat = jnp.concatenate([kp_ref[:, hs], kc_ref[:, hs]], axis=0)
        vcat = jnp.concatenate([vp_ref[:, hs], vc_ref[:, hs]], axis=0)
        s = lax.dot_general(q_ref[:, hs], kcat, (((1,), (1,)), ((), ())), preferred_element_type=F32) * scale
        s = jnp.where(valid, s - float(slopes[h]) * dist, -jnp.inf)
        m = jnp.max(s, axis=-1, keepdims=True)
        p = jnp.exp(s - m)
        den = jnp.sum(p, axis=-1, keepdims=True)
        o = jnp.dot(p.astype(BF16), vcat, preferred_element_type=F32) / den
        o_ref[h, out_rows, :] = o
        lse = jnp.where(lane == h, m + jnp.log(den), lse)
    l_ref[out_rows, :] = lse


def _attention_group(qkv, *, batch, seq, window, dilation, slopes):
    M, width = qkv.shape
    gw = HEADS_PER_GROUP * HEAD_DIM
    assert width == 3 * gw and seq % (dilation * Q_BLOCK) == 0
    nb = seq // (dilation * Q_BLOCK)
    blocks_per_batch = nb * dilation

    def cur(sec):
        return lambda b, n, r: (b * blocks_per_batch + n * dilation + r, sec)

    def prev(sec):
        return lambda b, n, r: (b * blocks_per_batch + jnp.maximum(n - 1, 0) * dilation + r, sec)

    blk = (Q_BLOCK, gw)
    chunk = dilation * Q_BLOCK
    return pl.pallas_call(
        functools.partial(_attn_body, dilation=dilation, span=window // dilation, slopes=slopes),
        grid=(batch, nb, dilation),
        in_specs=[pl.BlockSpec(blk, cur(0)), pl.BlockSpec(blk, cur(1)), pl.BlockSpec(blk, prev(1)),
                  pl.BlockSpec(blk, cur(2)), pl.BlockSpec(blk, prev(2))],
        out_specs=[pl.BlockSpec((HEADS_PER_GROUP, chunk, HEAD_DIM), lambda b, n, r: (0, b * nb + n, 0)),
                   pl.BlockSpec((chunk, HEAD_DIM), lambda b, n, r: (b * nb + n, 0))],
        out_shape=[jax.ShapeDtypeStruct((HEADS_PER_GROUP, M, HEAD_DIM), F32),
                   jax.ShapeDtypeStruct((M, HEAD_DIM), F32)],
        compiler_params=_params(3, 2 * (5 * Q_BLOCK * gw * 2 + chunk * (gw + HEAD_DIM) * 4) + (4 << 20)),
        name=f"attn_d{dilation}",
    )(qkv, qkv, qkv, qkv, qkv)


def _mix_body(y_ref, o1_ref, l1_ref, o2_ref, l2_ref, o3_ref, l3_ref, gr_ref, ga_ref, wr_ref, wa_ref,
              out_ref, *, row_chunk):
    for c in range(y_ref.shape[0] // row_chunk):
        rows = slice(c * row_chunk, (c + 1) * row_chunk)
        l1, l2, l3 = l1_ref[rows, :], l2_ref[rows, :], l3_ref[rows, :]
        m = jnp.maximum(jnp.maximum(l1, l2), l3)
        e1, e2, e3 = jnp.exp(l1 - m), jnp.exp(l2 - m), jnp.exp(l3 - m)
        den = e1 + e2 + e3
        w1, w2, w3 = e1 / den, e2 / den, e3 / den
        heads = []
        for h in range(HEADS_PER_GROUP):
            bc = lambda w: jnp.broadcast_to(w[:, h:h + 1], (row_chunk, HEAD_DIM))
            att = bc(w1) * o1_ref[h, rows, :] + bc(w2) * o2_ref[h, rows, :] + bc(w3) * o3_ref[h, rows, :]
            heads.append(att.astype(BF16))
        att = jnp.concatenate(heads, axis=1)
        ya = jnp.dot(y_ref[rows, :], wr_ref[...], preferred_element_type=F32)
        yb = jnp.dot(att, wa_ref[...], preferred_element_type=F32)
        out_ref[rows, :] = (gr_ref[rows, :].astype(F32) * ya + ga_ref[rows, :].astype(F32) * yb).astype(out_ref.dtype)


def _mix(y_rec, att_parts, gates, w_rnn, w_att, *, tm, row_chunk):
    M, d_rnn = y_rec.shape
    aw, D = w_att.shape
    rowblk = lambda w: pl.BlockSpec((tm, w), lambda i: (i, 0))
    headblk = pl.BlockSpec((HEADS_PER_GROUP, tm, HEAD_DIM), lambda i: (0, i, 0))
    resident = lambda r: pl.BlockSpec((r, D), lambda i: (0, 0), pipeline_mode=pl.Buffered(1))
    vmem = (2 * (tm * d_rnn * 2 + 3 * tm * (aw + HEAD_DIM) * 4 + 2 * tm * D * 2 + tm * D * 2) + (d_rnn + aw) * D * 2
            + 4 * row_chunk * D * 4)
    return pl.pallas_call(
        functools.partial(_mix_body, row_chunk=row_chunk),
        grid=(M // tm,),
        in_specs=[rowblk(d_rnn)] + [headblk, rowblk(HEAD_DIM)] * 3 + [
            pl.BlockSpec((tm, D), lambda i: (i, 0)),
            pl.BlockSpec((tm, D), lambda i: (i, 1)),
            resident(d_rnn), resident(aw)],
        out_specs=rowblk(D),
        out_shape=jax.ShapeDtypeStruct((M, D), BF16),
        compiler_params=_params(1, vmem),
        name="mix",
    )(y_rec, *att_parts, gates, gates, w_rnn, w_att)


def _outproj_body(m_ref, w_ref, x_ref, g_ref, x1_ref, h2_ref, *, row_chunk):
    for c in range(m_ref.shape[0] // row_chunk):
        rows = slice(c * row_chunk, (c + 1) * row_chunk)
        x1 = x_ref[rows, :] + jnp.dot(m_ref[rows, :], w_ref[...], preferred_element_type=F32)
        x1_ref[rows, :] = x1
        h2_ref[rows, :] = _rms(x1, g_ref[...]).astype(h2_ref.dtype)


def _outproj(mixed, w_out, x, g2, *, tm, row_chunk):
    M, D = x.shape
    vmem = 2 * (tm * D * 2 + D * D * 2 + tm * D * 4 + tm * D * 4 + tm * D * 2) + 2 * tm * D * 4
    return pl.pallas_call(
        functools.partial(_outproj_body, row_chunk=row_chunk),
        grid=(M // tm,),
        in_specs=[pl.BlockSpec((tm, D), lambda i: (i, 0)), pl.BlockSpec((D, D), lambda i: (0, 0)),
                  pl.BlockSpec((tm, D), lambda i: (i, 0)), pl.BlockSpec((1, D), lambda i: (0, 0))],
        out_specs=[pl.BlockSpec((tm, D), lambda i: (i, 0))] * 2,
        out_shape=[jax.ShapeDtypeStruct((M, D), F32), jax.ShapeDtypeStruct((M, D), BF16)],
        compiler_params=_params(1, vmem),
        name="outproj",
    )(mixed, w_out, x, g2)


def _ffn_up_body(h_ref, wg_ref, wv_ref, cw_ref, cb_ref, o_ref, gs_ref, halo_ref, *, tiles_per_batch, row_chunk):
    i = pl.program_id(0)
    j = pl.program_id(1)
    tm, tn = o_ref.shape

    @pl.when(i % tiles_per_batch == 0)
    def _():
        gs_ref[0:V7X_SUBLANES, :] = jnp.zeros((V7X_SUBLANES, tn), F32)

    @pl.when(i % tiles_per_batch != 0)
    def _():
        gs_ref[0:V7X_SUBLANES, :] = halo_ref[j]

    for c in range(tm // row_chunk):
        rows = slice(c * row_chunk, (c + 1) * row_chunk)
        base = V7X_SUBLANES + c * row_chunk
        h = h_ref[rows, :]
        gs_ref[base:base + row_chunk, :] = jnp.dot(h, wg_ref[...], preferred_element_type=F32)
        val = jnp.dot(h, wv_ref[...], preferred_element_type=F32)
        gate = cb_ref[...] + cw_ref[CONV_FFN - 1:CONV_FFN, :] * gs_ref[base:base + row_chunk, :]
        for t in range(CONV_FFN - 1):
            r0 = base - (CONV_FFN - 1) + t
            gate = gate + cw_ref[t:t + 1, :] * gs_ref[r0:r0 + row_chunk, :]
        o_ref[rows, :] = (jax.nn.gelu(gate) * val).astype(o_ref.dtype)
    halo_ref[j] = gs_ref[tm:tm + V7X_SUBLANES, :]


def _ffn_up(h2, w_up, conv_w, conv_b, *, seq, tm, tn, row_chunk):
    M, D = h2.shape
    d_ff = w_up.shape[1] // 2
    nj = d_ff // tn
    vmem = (2 * (tm * D * 2 + 2 * D * tn * 2 + tm * tn * 2) + (tm + 8) * tn * 4 + nj * 8 * tn * 4
            + 6 * row_chunk * tn * 4)
    return pl.pallas_call(
        functools.partial(_ffn_up_body, tiles_per_batch=seq // tm, row_chunk=row_chunk),
        grid=(M // tm, nj),
        in_specs=[pl.BlockSpec((tm, D), lambda i, j: (i, 0)),
                  pl.BlockSpec((D, tn), lambda i, j: (0, j)),
                  pl.BlockSpec((D, tn), lambda i, j: (0, j + nj)),
                  pl.BlockSpec((CONV_FFN, tn), lambda i, j: (0, j)),
                  pl.BlockSpec((1, tn), lambda i, j: (0, j))],
        out_specs=pl.BlockSpec((tm, tn), lambda i, j: (i, j)),
        out_shape=jax.ShapeDtypeStruct((M, d_ff), BF16),
        scratch_shapes=[pltpu.VMEM((tm + V7X_SUBLANES, tn), F32),
                        pltpu.VMEM((nj, V7X_SUBLANES, tn), F32)],
        compiler_params=_params(2, vmem),
        name="ffn_up",
    )(h2, w_up, w_up, conv_w, conv_b)


def _ffn_down_body(a_ref, w_ref, x_ref, g_ref, o_ref, *, final_norm, row_chunk):
    for c in range(a_ref.shape[0] // row_chunk):
        rows = slice(c * row_chunk, (c + 1) * row_chunk)
        x2 = x_ref[rows, :] + jnp.dot(a_ref[rows, :], w_ref[...], preferred_element_type=F32)
        o_ref[rows, :] = _rms(x2, g_ref[...]) if final_norm else x2


def _ffn_down(a, w_down, x1, g, *, tm, row_chunk, final_norm):
    M, d_ff = a.shape
    D = w_down.shape[1]
    vmem = 2 * (tm * d_ff * 2 + 2 * tm * D * 4) + d_ff * D * 2 + 2 * tm * D * 4
    return pl.pallas_call(
        functools.partial(_ffn_down_body, final_norm=final_norm, row_chunk=row_chunk),
        grid=(M // tm,),
        in_specs=[pl.BlockSpec((tm, d_ff), lambda i: (i, 0)),
                  pl.BlockSpec((d_ff, D), lambda i: (0, 0), pipeline_mode=pl.Buffered(1)),
                  pl.BlockSpec((tm, D), lambda i: (i, 0)),
                  pl.BlockSpec((1, D), lambda i: (0, 0))],
        out_specs=pl.BlockSpec((tm, D), lambda i: (i, 0)),
        out_shape=jax.ShapeDtypeStruct((M, D), F32),
        compiler_params=_params(1, vmem),
        name="ffn_down",
    )(a, w_down, x1, g)


def _layer(x2d, p, *, batch, seq):
    d_rnn = p["conv_w"].shape[1]
    gw = HEADS_PER_GROUP * HEAD_DIM
    att_w = len(ATT_GROUPS) * gw
    off_q = 2 * d_rnn
    off_g = off_q + 3 * att_w
    w_in = p["w_in"].astype(BF16)

    h = _norm(x2d, p["norm1_g"][None, :], tm=512)
    gates = _matmul(h, w_in[:, off_g:], tm=2048, tn=1024, act="sigmoid", row_chunk=512, name="inproj_gates")

    band = _lru_band(d_rnn, d_rnn // LRU_BLOCKS)
    y_rec = _lru(h, w_in[:, :off_q], p["conv_w"], p["conv_b"][None, :], _band_tiles(p["lru_wr"], band),
                 _band_tiles(p["lru_wi"], band), p["lru_br"][None, :], p["lru_bi"][None, :], p["lru_lambda"][None, :],
                 seq=seq, ts=256, band=band)

    slopes = _alibi_slopes(len(ATT_GROUPS) * HEADS_PER_GROUP).reshape(len(ATT_GROUPS), HEADS_PER_GROUP)
    att_parts = []
    for g, (window, dilation) in enumerate(ATT_GROUPS):
        w_g = jnp.concatenate([w_in[:, off_q + sec * att_w + g * gw: off_q + sec * att_w + (g + 1) * gw]
                               for sec in range(3)], axis=1)
        qkv = _matmul(h, w_g, tm=2048, tn=1024, dilation=dilation, name=f"inproj_qkv_d{dilation}")
        att_parts += _attention_group(qkv, batch=batch, seq=seq, window=window, dilation=dilation,
                                      slopes=tuple(float(s) for s in slopes[g]))

    mixed = _mix(y_rec, att_parts, gates, p["w_rnn_out"].astype(BF16), p["w_att_out"].astype(BF16),
                 tm=512, row_chunk=256)
    x1, h2 = _outproj(mixed, p["w_out"].astype(BF16), x2d, p["norm2_g"][None, :], tm=512, row_chunk=256)
    hmid = _ffn_up(h2, p["w_up"].astype(BF16), p["ffn_conv_w"], p["ffn_conv_b"][None, :], seq=seq,
                   tm=2048, tn=512, row_chunk=256)
    return x1, hmid


def kernel(x, norm1_g, w_in, conv_w, conv_b, lru_wr, lru_br, lru_wi, lru_bi, lru_lambda, w_rnn_out, w_att_out, w_out,
           norm2_g, w_up, ffn_conv_w, ffn_conv_b, w_down, final_g):
    B, S, D = x.shape
    depth = w_in.shape[0]
    x2d = x.reshape(B * S, D)
    for l in range(depth):
        p = dict(norm1_g=norm1_g[l], w_in=w_in[l], conv_w=conv_w[l], conv_b=conv_b[l], lru_wr=lru_wr[l],
                 lru_br=lru_br[l], lru_wi=lru_wi[l], lru_bi=lru_bi[l], lru_lambda=lru_lambda[l],
                 w_rnn_out=w_rnn_out[l], w_att_out=w_att_out[l], w_out=w_out[l], norm2_g=norm2_g[l],
                 w_up=w_up[l], ffn_conv_w=ffn_conv_w[l], ffn_conv_b=ffn_conv_b[l])
        x1, hmid = _layer(x2d, p, batch=B, seq=S)
        x2d = _ffn_down(hmid, w_down[l].astype(BF16), x1, final_g[None, :], tm=256, row_chunk=128,
                        final_norm=l == depth - 1)
    return x2d.reshape(B, S, D)
```

```python
import functools
import math

import numpy as np
import jax
import jax.numpy as jnp
from jax import lax
from jax.experimental import pallas as pl
from jax.experimental.pallas import tpu as pltpu

LRU_BLOCKS = 16
LRU_C = 8.0
CONV_REC = 4
ATT_GROUPS = ((128, 1), (512, 4), (2048, 16))
HEADS_PER_GROUP = 8
HEAD_DIM = 128
Q_BLOCK = 128
CONV_FFN = 3
EPS = 1e-6

V7X_LANES = 128
V7X_SUBLANES = 8
V7X_MXU_DIM = 256
V7X_VMEM_BYTES = 64 * 1024 * 1024

F32 = jnp.float32
BF16 = jnp.bfloat16


def _params(n_axes, vmem_bytes):
    limit = min(int(vmem_bytes * 1.25) + (4 << 20), V7X_VMEM_BYTES - (4 << 20))
    return pltpu.CompilerParams(dimension_semantics=("arbitrary",) * n_axes, vmem_limit_bytes=limit)


def _rms(xf, g):
    return xf * lax.rsqrt(jnp.mean(xf * xf, axis=-1, keepdims=True) + EPS) * g


def _norm_gates_body(x_ref, g_ref, w_ref, h_ref, o_ref, *, row_chunk):
    @pl.when(pl.program_id(1) == 0)
    def _():
        for c in range(x_ref.shape[0] // row_chunk):
            rows = slice(c * row_chunk, (c + 1) * row_chunk)
            h_ref[rows, :] = _rms(x_ref[rows, :], g_ref[...]).astype(h_ref.dtype)

    for c in range(x_ref.shape[0] // row_chunk):
        rows = slice(c * row_chunk, (c + 1) * row_chunk)
        acc = jnp.dot(h_ref[rows, :], w_ref[...], preferred_element_type=F32)
        o_ref[rows, :] = jax.nn.sigmoid(acc).astype(o_ref.dtype)


def _norm_gates(x, g, w, *, tm, tn, row_chunk):
    M, D = x.shape
    N = w.shape[1]
    vmem = 2 * (tm * D * 4 + D * tn * 2 + tm * D * 2 + tm * tn * 2) + 2 * row_chunk * max(D, tn) * 4
    return pl.pallas_call(
        functools.partial(_norm_gates_body, row_chunk=row_chunk),
        grid=(M // tm, N // tn),
        in_specs=[pl.BlockSpec((tm, D), lambda i, j: (i, 0)), pl.BlockSpec((1, D), lambda i, j: (0, 0)),
                  pl.BlockSpec((D, tn), lambda i, j: (0, j))],
        out_specs=[pl.BlockSpec((tm, D), lambda i, j: (i, 0)), pl.BlockSpec((tm, tn), lambda i, j: (i, j))],
        out_shape=[jax.ShapeDtypeStruct((M, D), BF16), jax.ShapeDtypeStruct((M, N), BF16)],
        compiler_params=_params(2, vmem),
        name="norm1_gates",
    )(x, g, w)


def _mm_body(a_ref, w_ref, o_ref):
    o_ref[...] = jnp.dot(a_ref[...], w_ref[...], preferred_element_type=F32).astype(o_ref.dtype)


BF16_TILE_ROWS = 16


def _subseq_perm(dilation):
    lb = BF16_TILE_ROWS * dilation
    assert V7X_MXU_DIM % lb == 0
    p = np.arange(V7X_MXU_DIM)
    q = p % lb
    perm = np.zeros((V7X_MXU_DIM, V7X_MXU_DIM), np.float32)
    perm[p, p // lb * lb + (q % BF16_TILE_ROWS) * dilation + q // BF16_TILE_ROWS] = 1.0
    return perm


def _mm_subseq_body(a_ref, p_ref, w_ref, o_ref, asub_ref, *, dilation):
    tm = a_ref.shape[0]
    lb = BF16_TILE_ROWS * dilation
    chunk = dilation * Q_BLOCK

    @pl.when(pl.program_id(1) == 0)
    def _():
        for blk in range(tm // V7X_MXU_DIM):
            t = jnp.dot(p_ref[...], a_ref[blk * V7X_MXU_DIM:(blk + 1) * V7X_MXU_DIM, :],
                        preferred_element_type=F32).astype(BF16)
            for l in range(V7X_MXU_DIM // lb):
                g = blk * (V7X_MXU_DIM // lb) + l
                base = (g // V7X_SUBLANES) * chunk + (g % V7X_SUBLANES) * BF16_TILE_ROWS
                for r in range(dilation):
                    src = l * lb + r * BF16_TILE_ROWS
                    dst = base + r * Q_BLOCK
                    asub_ref[dst:dst + BF16_TILE_ROWS, :] = t[src:src + BF16_TILE_ROWS, :]

    o_ref[...] = jnp.dot(asub_ref[...], w_ref[...], preferred_element_type=F32).astype(o_ref.dtype)


def _matmul(a, w, *, tm, tn, name, dilation=1):
    M, K = a.shape
    N = w.shape[1]
    assert M % tm == 0 and N % tn == 0
    vmem = 2 * (tm * K * 2 + K * tn * 2 + tm * tn * 2) + tm * tn * 4
    a_spec = pl.BlockSpec((tm, K), lambda i, j: (i, 0))
    w_spec = pl.BlockSpec((K, tn), lambda i, j: (0, j))
    if dilation == 1:
        body = _mm_body
        in_specs, operands, scratch = [a_spec, w_spec], (a, w), []
    else:
        assert tm % (dilation * Q_BLOCK) == 0
        body = functools.partial(_mm_subseq_body, dilation=dilation)
        perm = jnp.asarray(_subseq_perm(dilation), BF16)
        in_specs = [a_spec, pl.BlockSpec(perm.shape, lambda i, j: (0, 0)), w_spec]
        operands = (a, perm, w)
        scratch = [pltpu.VMEM((tm, K), BF16)]
        vmem += tm * K * 2
    return pl.pallas_call(
        body,
        grid=(M // tm, N // tn),
        in_specs=in_specs,
        out_specs=pl.BlockSpec((tm, tn), lambda i, j: (i, j)),
        out_shape=jax.ShapeDtypeStruct((M, N), BF16),
        scratch_shapes=scratch,
        compiler_params=_params(2, vmem),
        name=name,
    )(*operands)


def _lru_band(d_rnn, block):
    band = []
    for c0 in range(0, d_rnn, V7X_MXU_DIM):
        cn = min(V7X_MXU_DIM, d_rnn - c0)
        b0, b1 = c0 // block, (c0 + cn - 1) // block
        r0 = (block * b0) // V7X_LANES * V7X_LANES
        r1 = min(-(-(block * (b1 + 1)) // V7X_LANES) * V7X_LANES, d_rnn)
        band.append((c0, cn, r0, r1 - r0))
    return tuple(band)


def _band_tiles(w, band):
    dense = jax.scipy.linalg.block_diag(*[w[n] for n in range(w.shape[0])])
    kmax = max(k for _, _, _, k in band)
    tiles = []
    for c0, cn, r0, k in band:
        t = dense[r0:r0 + k, c0:c0 + cn]
        tiles.append(jnp.pad(t, ((0, kmax - k), (0, V7X_MXU_DIM - cn))))
    return jnp.stack(tiles).astype(BF16)


def _segment_order(ts):
    p = np.arange(ts)
    perm = np.zeros((ts, ts), np.float32)
    perm[p, (p % V7X_SUBLANES) * (ts // V7X_SUBLANES) + p // V7X_SUBLANES] = 1.0
    return perm


def _shift_sublanes(v, s, fill):
    row = lax.broadcasted_iota(jnp.int32, v.shape, 0)
    return jnp.where(row < s, fill, pltpu.roll(v, s, axis=0))


_LRU_HALO = (CONV_REC - 1) * V7X_SUBLANES


def _lru_body(h_ref, pin_ref, pout_ref, w_ref, cw_ref, cb_ref, wr_ref, wi_ref, br_ref, bi_ref, lam_ref, y_ref,
              hil_ref, xs_ref, xf_ref, xb_ref, gs_ref, tail_ref, hc_ref, *, ts, tiles_per_batch, band):
    d_rnn = y_ref.shape[1]
    nseg = V7X_SUBLANES
    seg_len = ts // nseg

    @pl.when(pl.program_id(0) % tiles_per_batch == 0)
    def _():
        tail_ref[...] = jnp.zeros(tail_ref.shape, F32)
        hc_ref[...] = jnp.zeros(hc_ref.shape, F32)

    hil_ref[...] = jnp.dot(pin_ref[...], h_ref[...], preferred_element_type=F32).astype(BF16)

    row8 = lax.broadcasted_iota(jnp.int32, (V7X_SUBLANES, V7X_MXU_DIM), 0)

    def project(c0, cn):
        cs = slice(c0, c0 + cn)
        xs_ref[_LRU_HALO:_LRU_HALO + ts, cs] = jnp.dot(hil_ref[...], w_ref[:, cs], preferred_element_type=F32)
        gs_ref[:, cs] = jnp.dot(hil_ref[...], w_ref[:, d_rnn + c0:d_rnn + c0 + cn], preferred_element_type=F32)
        for m in range(CONV_REC - 1):
            r0 = ts + m * V7X_SUBLANES
            cur8 = pltpu.roll(xs_ref[r0:r0 + V7X_SUBLANES, cs], 1, axis=0)
            prev8 = pltpu.roll(tail_ref[m * V7X_SUBLANES:(m + 1) * V7X_SUBLANES, cs], 1, axis=0)
            xs_ref[m * V7X_SUBLANES:(m + 1) * V7X_SUBLANES, cs] = jnp.where(row8[:, :cn] == 0, prev8, cur8)
        tail_ref[:, cs] = xs_ref[ts:ts + _LRU_HALO, cs]
        acc = cb_ref[:, cs] + cw_ref[CONV_REC - 1:CONV_REC, cs] * xs_ref[_LRU_HALO:_LRU_HALO + ts, cs]
        for i in range(CONV_REC - 1):
            r0 = i * V7X_SUBLANES
            acc = acc + cw_ref[i:i + 1, cs] * xs_ref[r0:r0 + ts, cs]
        xf_ref[:, cs] = acc
        xb_ref[:, cs] = acc.astype(BF16)

    def emit(cols, y_il):
        y_ref[:, cols] = jnp.dot(pout_ref[...], y_il, preferred_element_type=F32).astype(y_ref.dtype)

    projected = 0
    pending = None
    for j, (c0, cn, r0, k) in enumerate(band):
        ahead = band[min(j + 1, len(band) - 1)]
        while projected < len(band) and band[projected][0] < ahead[2] + ahead[3]:
            project(*band[projected][:2])
            projected += 1
        cs = slice(c0, c0 + cn)
        win = xb_ref[:, r0:r0 + k]
        zr = jnp.dot(win, wr_ref[j, :k, :], preferred_element_type=F32)[:, :cn] + br_ref[:, cs]
        zi = jnp.dot(win, wi_ref[j, :k, :], preferred_element_type=F32)[:, :cn] + bi_ref[:, cs]
        if pending is not None:
            emit(*pending)
        r = jax.nn.sigmoid(zr)
        gi = jax.nn.sigmoid(zi)
        log_a = -LRU_C * r * jax.nn.softplus(-lam_ref[:, cs])
        a = jnp.exp(log_a)
        b = jnp.sqrt(1.0 - jnp.exp(2.0 * log_a)) * (gi * xf_ref[:, cs])
        v = lambda x, i: x[i * V7X_SUBLANES:(i + 1) * V7X_SUBLANES]
        h_loc, a_cum = [v(b, 0)], [v(a, 0)]
        for i in range(1, seg_len):
            h_loc.append(v(a, i) * h_loc[-1] + v(b, i))
            a_cum.append(v(a, i) * a_cum[-1])
        a_inc, h_inc = a_cum[-1], h_loc[-1]
        s = 1
        while s < nseg:
            h_inc = a_inc * _shift_sublanes(h_inc, s, 0.0) + h_inc
            a_inc = a_inc * _shift_sublanes(a_inc, s, 1.0)
            s *= 2
        c_in = hc_ref[0:1, cs]
        c_end = a_inc * c_in + h_inc
        c_seg = _shift_sublanes(c_end, 1, c_in)
        hc_ref[0:1, cs] = c_end[nseg - 1:nseg, :]
        h = jnp.concatenate([a_cum[i] * c_seg + h_loc[i] for i in range(seg_len)], axis=0)
        pending = (cs, (h * jax.nn.gelu(gs_ref[:, cs])).astype(BF16))
    emit(*pending)


def _lru(h, w_rec, conv_w, conv_b, wr_t, wi_t, br, bi, lam, *, seq, ts, band):
    M, D = h.shape
    d_rnn = w_rec.shape[1] // 2
    nt, kmax, _ = wr_t.shape
    perm = _segment_order(ts)
    row = lambda i: (0, 0)
    resident = lambda shape: pl.BlockSpec(shape, lambda i: (0,) * len(shape), pipeline_mode=pl.Buffered(1))
    vmem = (D * 2 * d_rnn * 2 + 2 * nt * kmax * V7X_MXU_DIM * 2 + 2 * (ts * D * 2 + ts * d_rnn * 2) + ts * D * 2
            + (ts + _LRU_HALO) * d_rnn * 4 + ts * d_rnn * 10 + 24 * ts * V7X_MXU_DIM * 4)
    return pl.pallas_call(
        functools.partial(_lru_body, ts=ts, tiles_per_batch=seq // ts, band=band),
        grid=(M // ts,),
        in_specs=[
            pl.BlockSpec((ts, D), lambda i: (i, 0)),
            pl.BlockSpec((ts, ts), row),
            pl.BlockSpec((ts, ts), row),
            resident((D, 2 * d_rnn)),
            pl.BlockSpec((CONV_REC, d_rnn), row),
            pl.BlockSpec((1, d_rnn), row),
            resident((nt, kmax, V7X_MXU_DIM)),
            resident((nt, kmax, V7X_MXU_DIM)),
            pl.BlockSpec((1, d_rnn), row),
            pl.BlockSpec((1, d_rnn), row),
            pl.BlockSpec((1, d_rnn), row),
        ],
        out_specs=pl.BlockSpec((ts, d_rnn), lambda i: (i, 0)),
        out_shape=jax.ShapeDtypeStruct((M, d_rnn), BF16),
        scratch_shapes=[
            pltpu.VMEM((ts, D), BF16),
            pltpu.VMEM((ts + _LRU_HALO, d_rnn), F32),
            pltpu.VMEM((ts, d_rnn), F32),
            pltpu.VMEM((ts, d_rnn), BF16),
            pltpu.VMEM((ts, d_rnn), F32),
            pltpu.VMEM((_LRU_HALO, d_rnn), F32),
            pltpu.VMEM((V7X_SUBLANES, d_rnn), F32),
        ],
        compiler_params=_params(1, vmem),
        name="rglru",
    )(h, jnp.asarray(perm, BF16), jnp.asarray(perm.T, BF16), w_rec, conv_w, conv_b, wr_t, wi_t, br, bi, lam)


def _alibi_slopes(n):
    def pow2_slopes(m):
        start = 2.0 ** (-8.0 / m)
        return [start ** (i + 1) for i in range(m)]
    c = 2 ** int(math.floor(math.log2(n)))
    s = pow2_slopes(c) + pow2_slopes(2 * c)[0::2][: n - c]
    return np.sort(np.asarray(s, np.float32))[::-1].copy()


def _attn_bias(slopes, dilation, span):
    qi = np.arange(Q_BLOCK)[:, None]
    ki = np.arange(2 * Q_BLOCK)[None, :]
    steps = qi + Q_BLOCK - ki
    band = (steps >= 0) & (steps <= span)
    bias = -np.asarray(slopes, np.float32)[:, None, None] * (steps * dilation).astype(np.float32)[None]
    with_prev = np.where(band[None], bias, -np.inf)
    no_prev = np.where((band & (ki >= Q_BLOCK))[None], bias, -np.inf)
    return np.stack([no_prev, with_prev]).astype(np.float32)


def _attn_body(q_ref, kc_ref, kp_ref, vc_ref, vp_ref, bias_ref, o_ref, l_ref, *, dilation, nblk):
    n = pl.program_id(1)
    scale = HEAD_DIM ** -0.5
    lane = lax.broadcasted_iota(jnp.int32, (Q_BLOCK, HEAD_DIM), 1)
    for i in range(nblk):
        rows = slice(i * Q_BLOCK, (i + 1) * Q_BLOCK)
        if dilation == 1:
            has_prev = n > 0 if i == 0 else True
            out_rows = rows
        else:
            has_prev = n > 0
            out_rows = pl.ds(pl.program_id(2) * nblk + i, Q_BLOCK, stride=dilation)
        sel = 1 if has_prev is True else jnp.where(has_prev, 1, 0)
        lse = jnp.zeros((Q_BLOCK, HEAD_DIM), F32)
        for h in range(HEADS_PER_GROUP):
            hs = slice(h * HEAD_DIM, (h + 1) * HEAD_DIM)
            if dilation == 1 and i > 0:
                kcat = kc_ref[(i - 1) * Q_BLOCK:(i + 1) * Q_BLOCK, hs]
                vcat = vc_ref[(i - 1) * Q_BLOCK:(i + 1) * Q_BLOCK, hs]
            else:
                prev_rows = slice(0, Q_BLOCK) if dilation == 1 else rows
                kcat = jnp.concatenate([kp_ref[prev_rows, hs], kc_ref[rows, hs]], axis=0)
                vcat = jnp.concatenate([vp_ref[prev_rows, hs], vc_ref[rows, hs]], axis=0)
            s = lax.dot_general(q_ref[rows, hs], kcat, (((1,), (1,)), ((), ())), preferred_element_type=F32)
            s = s * scale + bias_ref[sel, h]
            m = jnp.max(s, axis=-1, keepdims=True)
            p = jnp.exp(s - m)
            den = jnp.sum(p, axis=-1, keepdims=True)
            o = jnp.dot(p.astype(BF16), vcat, preferred_element_type=F32) / den
            o_ref[h, out_rows, :] = o
            lse = jnp.where(lane == h, m + jnp.log(den), lse)
        l_ref[out_rows, :] = lse


def _attention_group(qkv, *, batch, seq, window, dilation, slopes, nblk):
    M, width = qkv.shape
    gw = HEADS_PER_GROUP * HEAD_DIM
    assert width == 3 * gw and seq % (dilation * Q_BLOCK) == 0
    nb = seq // (dilation * Q_BLOCK)
    blocks_per_batch = nb * dilation
    bias = jnp.asarray(_attn_bias(slopes, dilation, window // dilation))
    blk = (nblk * Q_BLOCK, gw)
    if dilation == 1:
        assert nb % nblk == 0
        grid = (batch, nb // nblk, 1)
        first = lambda b, n, r: b * blocks_per_batch + n * nblk
        cur = lambda sec: (lambda b, n, r: (first(b, n, r) // nblk, sec))
        prev = lambda sec: (lambda b, n, r: (jnp.maximum(first(b, n, r) - 1, 0), sec))
        prev_blk = (Q_BLOCK, gw)
        out_rows = nblk * Q_BLOCK
        out_idx = lambda b, n, r: b * (nb // nblk) + n
    else:
        assert dilation % nblk == 0
        grid = (batch, nb, dilation // nblk)
        first = lambda b, n, r: b * blocks_per_batch + n * dilation + r * nblk
        cur = lambda sec: (lambda b, n, r: (first(b, n, r) // nblk, sec))
        prev = lambda sec: (lambda b, n, r: (first(b, jnp.maximum(n - 1, 0), r) // nblk, sec))
        prev_blk = blk
        out_rows = dilation * Q_BLOCK
        out_idx = lambda b, n, r: b * nb + n
    vmem = (2 * (3 * nblk * Q_BLOCK + 2 * prev_blk[0]) * gw * 2 + 2 * out_rows * (gw + HEAD_DIM) * 4
            + 2 * bias.size * 4)
    return pl.pallas_call(
        functools.partial(_attn_body, dilation=dilation, nblk=nblk),
        grid=grid,
        in_specs=[pl.BlockSpec(blk, cur(0)), pl.BlockSpec(blk, cur(1)), pl.BlockSpec(prev_blk, prev(1)),
                  pl.BlockSpec(blk, cur(2)), pl.BlockSpec(prev_blk, prev(2)),
                  pl.BlockSpec(bias.shape, lambda b, n, r: (0, 0, 0, 0))],
        out_specs=[pl.BlockSpec((HEADS_PER_GROUP, out_rows, HEAD_DIM), lambda b, n, r: (0, out_idx(b, n, r), 0)),
                   pl.BlockSpec((out_rows, HEAD_DIM), lambda b, n, r: (out_idx(b, n, r), 0))],
        out_shape=[jax.ShapeDtypeStruct((HEADS_PER_GROUP, M, HEAD_DIM), F32),
                   jax.ShapeDtypeStruct((M, HEAD_DIM), F32)],
        compiler_params=_params(3, vmem + (4 << 20)),
        name=f"attn_d{dilation}",
    )(qkv, qkv, qkv, qkv, qkv, bias)


def _mix_body(y_ref, o1_ref, l1_ref, o2_ref, l2_ref, o3_ref, l3_ref, gr_ref, ga_ref, wr_ref, wa_ref,
              out_ref, *, row_chunk):
    for c in range(y_ref.shape[0] // row_chunk):
        rows = slice(c * row_chunk, (c + 1) * row_chunk)
        l1, l2, l3 = l1_ref[rows, :], l2_ref[rows, :], l3_ref[rows, :]
        m = jnp.maximum(jnp.maximum(l1, l2), l3)
        e1, e2, e3 = jnp.exp(l1 - m), jnp.exp(l2 - m), jnp.exp(l3 - m)
        den = e1 + e2 + e3
        w1, w2, w3 = e1 / den, e2 / den, e3 / den
        heads = []
        for h in range(HEADS_PER_GROUP):
            bc = lambda w: jnp.broadcast_to(w[:, h:h + 1], (row_chunk, HEAD_DIM))
            att = bc(w1) * o1_ref[h, rows, :] + bc(w2) * o2_ref[h, rows, :] + bc(w3) * o3_ref[h, rows, :]
            heads.append(att.astype(BF16))
        att = jnp.concatenate(heads, axis=1)
        ya = jnp.dot(y_ref[rows, :], wr_ref[...], preferred_element_type=F32)
        yb = jnp.dot(att, wa_ref[...], preferred_element_type=F32)
        out_ref[rows, :] = (gr_ref[rows, :].astype(F32) * ya + ga_ref[rows, :].astype(F32) * yb).astype(out_ref.dtype)


def _mix(y_rec, att_parts, gates, w_rnn, w_att, *, tm, row_chunk):
    M, d_rnn = y_rec.shape
    aw, D = w_att.shape
    rowblk = lambda w: pl.BlockSpec((tm, w), lambda i: (i, 0))
    headblk = pl.BlockSpec((HEADS_PER_GROUP, tm, HEAD_DIM), lambda i: (0, i, 0))
    resident = lambda r: pl.BlockSpec((r, D), lambda i: (0, 0), pipeline_mode=pl.Buffered(1))
    vmem = (2 * (tm * d_rnn * 2 + 3 * tm * (aw + HEAD_DIM) * 4 + 2 * tm * D * 2 + tm * D * 2) + (d_rnn + aw) * D * 2
            + 4 * row_chunk * D * 4)
    return pl.pallas_call(
        functools.partial(_mix_body, row_chunk=row_chunk),
        grid=(M // tm,),
        in_specs=[rowblk(d_rnn)] + [headblk, rowblk(HEAD_DIM)] * 3 + [
            pl.BlockSpec((tm, D), lambda i: (i, 0)),
            pl.BlockSpec((tm, D), lambda i: (i, 1)),
            resident(d_rnn), resident(aw)],
        out_specs=rowblk(D),
        out_shape=jax.ShapeDtypeStruct((M, D), BF16),
        compiler_params=_params(1, vmem),
        name="mix",
    )(y_rec, *att_parts, gates, gates, w_rnn, w_att)


def _outproj_body(m_ref, w_ref, x_ref, g_ref, x1_ref, h2_ref, *, row_chunk):
    for c in range(m_ref.shape[0] // row_chunk):
        rows = slice(c * row_chunk, (c + 1) * row_chunk)
        x1 = x_ref[rows, :] + jnp.dot(m_ref[rows, :], w_ref[...], preferred_element_type=F32)
        x1_ref[rows, :] = x1
        h2_ref[rows, :] = _rms(x1, g_ref[...]).astype(h2_ref.dtype)


def _outproj(mixed, w_out, x, g2, *, tm, row_chunk):
    M, D = x.shape
    vmem = 2 * (tm * D * 2 + D * D * 2 + tm * D * 4 + tm * D * 4 + tm * D * 2) + 2 * tm * D * 4
    return pl.pallas_call(
        functools.partial(_outproj_body, row_chunk=row_chunk),
        grid=(M // tm,),
        in_specs=[pl.BlockSpec((tm, D), lambda i: (i, 0)), pl.BlockSpec((D, D), lambda i: (0, 0)),
                  pl.BlockSpec((tm, D), lambda i: (i, 0)), pl.BlockSpec((1, D), lambda i: (0, 0))],
        out_specs=[pl.BlockSpec((tm, D), lambda i: (i, 0))] * 2,
        out_shape=[jax.ShapeDtypeStruct((M, D), F32), jax.ShapeDtypeStruct((M, D), BF16)],
        compiler_params=_params(1, vmem),
        name="outproj",
    )(mixed, w_out, x, g2)


def _ffn_up_body(h_ref, wg_ref, wv_ref, cw_ref, cb_ref, o_ref, gs_ref, vs_ref, halo_ref, *, tiles_per_batch,
                 row_chunk):
    i = pl.program_id(0)
    j = pl.program_id(1)
    tm, tn = o_ref.shape

    @pl.when(i % tiles_per_batch == 0)
    def _():
        gs_ref[0:V7X_SUBLANES, :] = jnp.zeros((V7X_SUBLANES, tn), F32)

    @pl.when(i % tiles_per_batch != 0)
    def _():
        gs_ref[0:V7X_SUBLANES, :] = halo_ref[j]

    for c in range(tm // row_chunk):
        rows = slice(c * row_chunk, (c + 1) * row_chunk)
        base = V7X_SUBLANES + c * row_chunk
        h = h_ref[rows, :]
        gs_ref[base:base + row_chunk, :] = jnp.dot(h, wg_ref[...], preferred_element_type=F32)
        vs_ref[rows, :] = jnp.dot(h, wv_ref[...], preferred_element_type=F32)
        for s in range(tn // V7X_LANES):
            cs = slice(s * V7X_LANES, (s + 1) * V7X_LANES)
            gate = cb_ref[:, cs] + cw_ref[CONV_FFN - 1:CONV_FFN, cs] * gs_ref[base:base + row_chunk, cs]
            for t in range(CONV_FFN - 1):
                r0 = base - (CONV_FFN - 1) + t
                gate = gate + cw_ref[t:t + 1, cs] * gs_ref[r0:r0 + row_chunk, cs]
            o_ref[rows, cs] = (jax.nn.gelu(gate) * vs_ref[rows, cs]).astype(o_ref.dtype)
    halo_ref[j] = gs_ref[tm:tm + V7X_SUBLANES, :]


def _ffn_up(h2, w_up, conv_w, conv_b, *, seq, tm, tn, row_chunk):
    M, D = h2.shape
    d_ff = w_up.shape[1] // 2
    nj = d_ff // tn
    vmem = (2 * (tm * D * 2 + 2 * D * tn * 2 + tm * tn * 2) + (tm + 8) * tn * 4 + nj * 8 * tn * 4
            + 6 * row_chunk * tn * 4)
    return pl.pallas_call(
        functools.partial(_ffn_up_body, tiles_per_batch=seq // tm, row_chunk=row_chunk),
        grid=(M // tm, nj),
        in_specs=[pl.BlockSpec((tm, D), lambda i, j: (i, 0)),
                  pl.BlockSpec((D, tn), lambda i, j: (0, j)),
                  pl.BlockSpec((D, tn), lambda i, j: (0, j + nj)),
                  pl.BlockSpec((CONV_FFN, tn), lambda i, j: (0, j)),
                  pl.BlockSpec((1, tn), lambda i, j: (0, j))],
        out_specs=pl.BlockSpec((tm, tn), lambda i, j: (i, j)),
        out_shape=jax.ShapeDtypeStruct((M, d_ff), BF16),
        scratch_shapes=[pltpu.VMEM((tm + V7X_SUBLANES, tn), F32),
                        pltpu.VMEM((tm, tn), F32),
                        pltpu.VMEM((nj, V7X_SUBLANES, tn), F32)],
        compiler_params=_params(2, vmem),
        name="ffn_up",
    )(h2, w_up, w_up, conv_w, conv_b)


def _ffn_down_body(a_ref, w_ref, x_ref, g_ref, o_ref, *, final_norm, row_chunk):
    for c in range(a_ref.shape[0] // row_chunk):
        rows = slice(c * row_chunk, (c + 1) * row_chunk)
        x2 = x_ref[rows, :] + jnp.dot(a_ref[rows, :], w_ref[...], preferred_element_type=F32)
        o_ref[rows, :] = _rms(x2, g_ref[...]) if final_norm else x2


def _ffn_down(a, w_down, x1, g, *, tm, row_chunk, final_norm):
    M, d_ff = a.shape
    D = w_down.shape[1]
    vmem = 2 * (tm * d_ff * 2 + 2 * tm * D * 4) + d_ff * D * 2 + 2 * tm * D * 4
    return pl.pallas_call(
        functools.partial(_ffn_down_body, final_norm=final_norm, row_chunk=row_chunk),
        grid=(M // tm,),
        in_specs=[pl.BlockSpec((tm, d_ff), lambda i: (i, 0)),
                  pl.BlockSpec((d_ff, D), lambda i: (0, 0), pipeline_mode=pl.Buffered(1)),
                  pl.BlockSpec((tm, D), lambda i: (i, 0)),
                  pl.BlockSpec((1, D), lambda i: (0, 0))],
        out_specs=pl.BlockSpec((tm, D), lambda i: (i, 0)),
        out_shape=jax.ShapeDtypeStruct((M, D), F32),
        compiler_params=_params(1, vmem),
        name="ffn_down",
    )(a, w_down, x1, g)


def _layer(x2d, p, *, batch, seq):
    d_rnn = p["conv_w"].shape[1]
    gw = HEADS_PER_GROUP * HEAD_DIM
    att_w = len(ATT_GROUPS) * gw
    off_q = 2 * d_rnn
    off_g = off_q + 3 * att_w
    w_in = p["w_in"].astype(BF16)

    h, gates = _norm_gates(x2d, p["norm1_g"][None, :], w_in[:, off_g:], tm=1024, tn=1024, row_chunk=512)

    band = _lru_band(d_rnn, d_rnn // LRU_BLOCKS)
    y_rec = _lru(h, w_in[:, :off_q], p["conv_w"], p["conv_b"][None, :], _band_tiles(p["lru_wr"], band),
                 _band_tiles(p["lru_wi"], band), p["lru_br"][None, :], p["lru_bi"][None, :], p["lru_lambda"][None, :],
                 seq=seq, ts=256, band=band)

    slopes = _alibi_slopes(len(ATT_GROUPS) * HEADS_PER_GROUP).reshape(len(ATT_GROUPS), HEADS_PER_GROUP)
    att_parts = []
    for g, (window, dilation) in enumerate(ATT_GROUPS):
        w_g = jnp.concatenate([w_in[:, off_q + sec * att_w + g * gw: off_q + sec * att_w + (g + 1) * gw]
                               for sec in range(3)], axis=1)
        qkv = _matmul(h, w_g, tm=2048, tn=1024, dilation=dilation, name=f"inproj_qkv_d{dilation}")
        att_parts += _attention_group(qkv, batch=batch, seq=seq, window=window, dilation=dilation,
                                      slopes=slopes[g], nblk=4)

    mixed = _mix(y_rec, att_parts, gates, p["w_rnn_out"].astype(BF16), p["w_att_out"].astype(BF16),
                 tm=512, row_chunk=512)
    x1, h2 = _outproj(mixed, p["w_out"].astype(BF16), x2d, p["norm2_g"][None, :], tm=512, row_chunk=512)
    hmid = _ffn_up(h2, p["w_up"].astype(BF16), p["ffn_conv_w"], p["ffn_conv_b"][None, :], seq=seq,
                   tm=2048, tn=512, row_chunk=1024)
    return x1, hmid


def kernel(x, norm1_g, w_in, conv_w, conv_b, lru_wr, lru_br, lru_wi, lru_bi, lru_lambda, w_rnn_out, w_att_out, w_out,
           norm2_g, w_up, ffn_conv_w, ffn_conv_b, w_down, final_g):
    B, S, D = x.shape
    depth = w_in.shape[0]
    x2d = x.reshape(B * S, D)
    for l in range(depth):
        p = dict(norm1_g=norm1_g[l], w_in=w_in[l], conv_w=conv_w[l], conv_b=conv_b[l], lru_wr=lru_wr[l],
                 lru_br=lru_br[l], lru_wi=lru_wi[l], lru_bi=lru_bi[l], lru_lambda=lru_lambda[l],
                 w_rnn_out=w_rnn_out[l], w_att_out=w_att_out[l], w_out=w_out[l], norm2_g=norm2_g[l],
                 w_up=w_up[l], ffn_conv_w=ffn_conv_w[l], ffn_conv_b=ffn_conv_b[l])
        x1, hmid = _layer(x2d, p, batch=B, seq=S)
        x2d = _ffn_down(hmid, w_down[l].astype(BF16), x1, final_g[None, :], tm=256, row_chunk=256,
                        final_norm=l == depth - 1)
    return x2d.reshape(B, S, D)
```

```python
import functools
import math

import numpy as np
import jax
import jax.numpy as jnp
from jax import lax
from jax.experimental import pallas as pl
from jax.experimental.pallas import tpu as pltpu

LRU_BLOCKS = 16
LRU_C = 8.0
CONV_REC = 4
ATT_GROUPS = ((128, 1), (512, 4), (2048, 16))
HEADS_PER_GROUP = 8
HEAD_DIM = 128
Q_BLOCK = 128
CONV_FFN = 3
EPS = 1e-6

V7X_LANES = 128
V7X_SUBLANES = 8
V7X_MXU_DIM = 256
V7X_VMEM_BYTES = 64 * 1024 * 1024

F32 = jnp.float32
BF16 = jnp.bfloat16


def _params(n_axes, vmem_bytes):
    limit = min(int(vmem_bytes * 1.25) + (4 << 20), V7X_VMEM_BYTES - (4 << 20))
    return pltpu.CompilerParams(dimension_semantics=("arbitrary",) * n_axes, vmem_limit_bytes=limit)


def _rms(xf, g):
    return xf * lax.rsqrt(jnp.mean(xf * xf, axis=-1, keepdims=True) + EPS) * g


def _norm_gates_body(x_ref, g_ref, w_ref, h_ref, o_ref, *, row_chunk):
    @pl.when(pl.program_id(1) == 0)
    def _():
        for c in range(x_ref.shape[0] // row_chunk):
            rows = slice(c * row_chunk, (c + 1) * row_chunk)
            h_ref[rows, :] = _rms(x_ref[rows, :], g_ref[...]).astype(h_ref.dtype)

    for c in range(x_ref.shape[0] // row_chunk):
        rows = slice(c * row_chunk, (c + 1) * row_chunk)
        acc = jnp.dot(h_ref[rows, :], w_ref[...], preferred_element_type=F32)
        o_ref[rows, :] = jax.nn.sigmoid(acc).astype(o_ref.dtype)


def _norm_gates(x, g, w, *, tm, tn, row_chunk):
    M, D = x.shape
    N = w.shape[1]
    vmem = 2 * (tm * D * 4 + D * tn * 2 + tm * D * 2 + tm * tn * 2) + 2 * row_chunk * max(D, tn) * 4
    return pl.pallas_call(
        functools.partial(_norm_gates_body, row_chunk=row_chunk),
        grid=(M // tm, N // tn),
        in_specs=[pl.BlockSpec((tm, D), lambda i, j: (i, 0)), pl.BlockSpec((1, D), lambda i, j: (0, 0)),
                  pl.BlockSpec((D, tn), lambda i, j: (0, j))],
        out_specs=[pl.BlockSpec((tm, D), lambda i, j: (i, 0)), pl.BlockSpec((tm, tn), lambda i, j: (i, j))],
        out_shape=[jax.ShapeDtypeStruct((M, D), BF16), jax.ShapeDtypeStruct((M, N), BF16)],
        compiler_params=_params(2, vmem),
        name="norm1_gates",
    )(x, g, w)


def _mm_body(a_ref, w_ref, o_ref):
    o_ref[...] = jnp.dot(a_ref[...], w_ref[...], preferred_element_type=F32).astype(o_ref.dtype)


BF16_TILE_ROWS = 16


def _subseq_perm(dilation):
    lb = BF16_TILE_ROWS * dilation
    assert V7X_MXU_DIM % lb == 0
    p = np.arange(V7X_MXU_DIM)
    q = p % lb
    perm = np.zeros((V7X_MXU_DIM, V7X_MXU_DIM), np.float32)
    perm[p, p // lb * lb + (q % BF16_TILE_ROWS) * dilation + q // BF16_TILE_ROWS] = 1.0
    return perm


def _reorder_rows(a_ref, p_ref, asub_ref, dilation):
    tm = a_ref.shape[0]
    lb = BF16_TILE_ROWS * dilation
    chunk = dilation * Q_BLOCK
    for blk in range(tm // V7X_MXU_DIM):
        t = jnp.dot(p_ref[...], a_ref[blk * V7X_MXU_DIM:(blk + 1) * V7X_MXU_DIM, :],
                    preferred_element_type=F32).astype(BF16)
        for l in range(V7X_MXU_DIM // lb):
            g = blk * (V7X_MXU_DIM // lb) + l
            base = (g // V7X_SUBLANES) * chunk + (g % V7X_SUBLANES) * BF16_TILE_ROWS
            for r in range(dilation):
                src = l * lb + r * BF16_TILE_ROWS
                dst = base + r * Q_BLOCK
                asub_ref[dst:dst + BF16_TILE_ROWS, :] = t[src:src + BF16_TILE_ROWS, :]


def _matmul(a, w, *, tm, tn, name):
    M, K = a.shape
    N = w.shape[1]
    assert M % tm == 0 and N % tn == 0
    return pl.pallas_call(
        _mm_body,
        grid=(M // tm, N // tn),
        in_specs=[pl.BlockSpec((tm, K), lambda i, j: (i, 0)), pl.BlockSpec((K, tn), lambda i, j: (0, j))],
        out_specs=pl.BlockSpec((tm, tn), lambda i, j: (i, j)),
        out_shape=jax.ShapeDtypeStruct((M, N), BF16),
        compiler_params=_params(2, 2 * (tm * K * 2 + K * tn * 2 + tm * tn * 2) + tm * tn * 4),
        name=name,
    )(a, w)


def _lru_band(d_rnn, block):
    band = []
    for c0 in range(0, d_rnn, V7X_MXU_DIM):
        cn = min(V7X_MXU_DIM, d_rnn - c0)
        b0, b1 = c0 // block, (c0 + cn - 1) // block
        r0 = (block * b0) // V7X_LANES * V7X_LANES
        r1 = min(-(-(block * (b1 + 1)) // V7X_LANES) * V7X_LANES, d_rnn)
        band.append((c0, cn, r0, r1 - r0))
    return tuple(band)


def _band_tiles(w, band):
    dense = jax.scipy.linalg.block_diag(*[w[n] for n in range(w.shape[0])])
    kmax = max(k for _, _, _, k in band)
    tiles = []
    for c0, cn, r0, k in band:
        t = dense[r0:r0 + k, c0:c0 + cn]
        tiles.append(jnp.pad(t, ((0, kmax - k), (0, V7X_MXU_DIM - cn))))
    return jnp.stack(tiles).astype(BF16)


def _segment_order(ts):
    p = np.arange(ts)
    perm = np.zeros((ts, ts), np.float32)
    perm[p, (p % V7X_SUBLANES) * (ts // V7X_SUBLANES) + p // V7X_SUBLANES] = 1.0
    return perm


def _shift_sublanes(v, s, fill):
    row = lax.broadcasted_iota(jnp.int32, v.shape, 0)
    return jnp.where(row < s, fill, pltpu.roll(v, s, axis=0))


_LRU_HALO = (CONV_REC - 1) * V7X_SUBLANES


def _lru_body(h_ref, pin_ref, pout_ref, w_ref, cw_ref, cb_ref, wr_ref, wi_ref, br_ref, bi_ref, lam_ref, y_ref,
              hil_ref, xs_ref, xf_ref, xb_ref, gs_ref, tail_ref, hc_ref, *, ts, steps_per_batch, band):
    @pl.when(pl.program_id(0) % steps_per_batch == 0)
    def _():
        tail_ref[...] = jnp.zeros(tail_ref.shape, F32)
        hc_ref[...] = jnp.zeros(hc_ref.shape, F32)

    for t in range(h_ref.shape[0] // ts):
        rows = slice(t * ts, (t + 1) * ts)
        _lru_tile(h_ref.at[rows], pin_ref, pout_ref, w_ref, cw_ref, cb_ref, wr_ref, wi_ref, br_ref, bi_ref, lam_ref,
                  y_ref.at[rows], hil_ref.at[t], xs_ref.at[t], xf_ref.at[t], xb_ref.at[t], gs_ref.at[t],
                  tail_ref, hc_ref, ts=ts, band=band)


def _lru_tile(h_ref, pin_ref, pout_ref, w_ref, cw_ref, cb_ref, wr_ref, wi_ref, br_ref, bi_ref, lam_ref, y_ref,
              hil_ref, xs_ref, xf_ref, xb_ref, gs_ref, tail_ref, hc_ref, *, ts, band):
    d_rnn = y_ref.shape[1]
    nseg = V7X_SUBLANES
    seg_len = ts // nseg

    hil_ref[...] = jnp.dot(pin_ref[...], h_ref[...], preferred_element_type=F32).astype(BF16)

    row8 = lax.broadcasted_iota(jnp.int32, (V7X_SUBLANES, V7X_MXU_DIM), 0)

    def project(c0, cn):
        cs = slice(c0, c0 + cn)
        xs_ref[_LRU_HALO:_LRU_HALO + ts, cs] = jnp.dot(hil_ref[...], w_ref[:, cs], preferred_element_type=F32)
        gs_ref[:, cs] = jnp.dot(hil_ref[...], w_ref[:, d_rnn + c0:d_rnn + c0 + cn], preferred_element_type=F32)
        for m in range(CONV_REC - 1):
            r0 = ts + m * V7X_SUBLANES
            cur8 = pltpu.roll(xs_ref[r0:r0 + V7X_SUBLANES, cs], 1, axis=0)
            prev8 = pltpu.roll(tail_ref[m * V7X_SUBLANES:(m + 1) * V7X_SUBLANES, cs], 1, axis=0)
            xs_ref[m * V7X_SUBLANES:(m + 1) * V7X_SUBLANES, cs] = jnp.where(row8[:, :cn] == 0, prev8, cur8)
        tail_ref[:, cs] = xs_ref[ts:ts + _LRU_HALO, cs]
        acc = cb_ref[:, cs] + cw_ref[CONV_REC - 1:CONV_REC, cs] * xs_ref[_LRU_HALO:_LRU_HALO + ts, cs]
        for i in range(CONV_REC - 1):
            r0 = i * V7X_SUBLANES
            acc = acc + cw_ref[i:i + 1, cs] * xs_ref[r0:r0 + ts, cs]
        xf_ref[:, cs] = acc
        xb_ref[:, cs] = acc.astype(BF16)

    def emit(cols, y_il):
        y_ref[:, cols] = jnp.dot(pout_ref[...], y_il, preferred_element_type=F32).astype(y_ref.dtype)

    projected = 0
    pending = None
    for j, (c0, cn, r0, k) in enumerate(band):
        ahead = band[min(j + 1, len(band) - 1)]
        while projected < len(band) and band[projected][0] < ahead[2] + ahead[3]:
            project(*band[projected][:2])
            projected += 1
        cs = slice(c0, c0 + cn)
        win = xb_ref[:, r0:r0 + k]
        zr = jnp.dot(win, wr_ref[j, :k, :], preferred_element_type=F32)[:, :cn] + br_ref[:, cs]
        zi = jnp.dot(win, wi_ref[j, :k, :], preferred_element_type=F32)[:, :cn] + bi_ref[:, cs]
        if pending is not None:
            emit(*pending)
        r = jax.nn.sigmoid(zr)
        gi = jax.nn.sigmoid(zi)
        log_a = -LRU_C * r * jax.nn.softplus(-lam_ref[:, cs])
        a = jnp.exp(log_a)
        b = jnp.sqrt(1.0 - jnp.exp(2.0 * log_a)) * (gi * xf_ref[:, cs])
        v = lambda x, i: x[i * V7X_SUBLANES:(i + 1) * V7X_SUBLANES]
        h_loc, a_cum = [v(b, 0)], [v(a, 0)]
        for i in range(1, seg_len):
            h_loc.append(v(a, i) * h_loc[-1] + v(b, i))
            a_cum.append(v(a, i) * a_cum[-1])
        a_inc, h_inc = a_cum[-1], h_loc[-1]
        s = 1
        while s < nseg:
            h_inc = a_inc * _shift_sublanes(h_inc, s, 0.0) + h_inc
            a_inc = a_inc * _shift_sublanes(a_inc, s, 1.0)
            s *= 2
        c_in = hc_ref[0:1, cs]
        c_end = a_inc * c_in + h_inc
        c_seg = _shift_sublanes(c_end, 1, c_in)
        hc_ref[0:1, cs] = c_end[nseg - 1:nseg, :]
        h = jnp.concatenate([a_cum[i] * c_seg + h_loc[i] for i in range(seg_len)], axis=0)
        pending = (cs, (h * jax.nn.gelu(gs_ref[:, cs])).astype(BF16))
    emit(*pending)


def _lru(h, w_rec, conv_w, conv_b, wr_t, wi_t, br, bi, lam, *, seq, ts, tiles, band):
    M, D = h.shape
    d_rnn = w_rec.shape[1] // 2
    nt, kmax, _ = wr_t.shape
    perm = _segment_order(ts)
    rows = tiles * ts
    row = lambda i: (0, 0)
    resident = lambda shape: pl.BlockSpec(shape, lambda i: (0,) * len(shape), pipeline_mode=pl.Buffered(1))
    vmem = (D * 2 * d_rnn * 2 + 2 * nt * kmax * V7X_MXU_DIM * 2 + 2 * rows * (D + d_rnn) * 2
            + tiles * (ts * D * 2 + (ts + _LRU_HALO) * d_rnn * 4 + ts * d_rnn * 10) + 24 * ts * V7X_MXU_DIM * 4)
    return pl.pallas_call(
        functools.partial(_lru_body, ts=ts, steps_per_batch=seq // rows, band=band),
        grid=(M // rows,),
        in_specs=[
            pl.BlockSpec((rows, D), lambda i: (i, 0)),
            pl.BlockSpec((ts, ts), row),
            pl.BlockSpec((ts, ts), row),
            resident((D, 2 * d_rnn)),
            pl.BlockSpec((CONV_REC, d_rnn), row),
            pl.BlockSpec((1, d_rnn), row),
            resident((nt, kmax, V7X_MXU_DIM)),
            resident((nt, kmax, V7X_MXU_DIM)),
            pl.BlockSpec((1, d_rnn), row),
            pl.BlockSpec((1, d_rnn), row),
            pl.BlockSpec((1, d_rnn), row),
        ],
        out_specs=pl.BlockSpec((rows, d_rnn), lambda i: (i, 0)),
        out_shape=jax.ShapeDtypeStruct((M, d_rnn), BF16),
        scratch_shapes=[
            pltpu.VMEM((tiles, ts, D), BF16),
            pltpu.VMEM((tiles, ts + _LRU_HALO, d_rnn), F32),
            pltpu.VMEM((tiles, ts, d_rnn), F32),
            pltpu.VMEM((tiles, ts, d_rnn), BF16),
            pltpu.VMEM((tiles, ts, d_rnn), F32),
            pltpu.VMEM((_LRU_HALO, d_rnn), F32),
            pltpu.VMEM((V7X_SUBLANES, d_rnn), F32),
        ],
        compiler_params=_params(1, vmem),
        name="rglru",
    )(h, jnp.asarray(perm, BF16), jnp.asarray(perm.T, BF16), w_rec, conv_w, conv_b, wr_t, wi_t, br, bi, lam)


def _alibi_slopes(n):
    def pow2_slopes(m):
        start = 2.0 ** (-8.0 / m)
        return [start ** (i + 1) for i in range(m)]
    c = 2 ** int(math.floor(math.log2(n)))
    s = pow2_slopes(c) + pow2_slopes(2 * c)[0::2][: n - c]
    return np.sort(np.asarray(s, np.float32))[::-1].copy()


def _attn_bias(slopes, dilation, span):
    qi = np.arange(Q_BLOCK)[:, None]
    ki = np.arange(2 * Q_BLOCK)[None, :]
    steps = qi + Q_BLOCK - ki
    band = (steps >= 0) & (steps <= span)
    bias = -np.asarray(slopes, np.float32)[:, None, None] * (steps * dilation).astype(np.float32)[None]
    with_prev = np.where(band[None], bias, -np.inf)
    no_prev = np.where((band & (ki >= Q_BLOCK))[None], bias, -np.inf)
    return np.stack([no_prev, with_prev]).astype(np.float32)


def _attn_block(i, n, rstep, q_ref, kc_ref, kp_ref, vc_ref, vp_ref, bias_ref, o_ref, l_ref, *, dilation, nblk,
                filler=None):
    scale = HEAD_DIM ** -0.5
    lane = lax.broadcasted_iota(jnp.int32, (Q_BLOCK, HEAD_DIM), 1)
    rows = slice(i * Q_BLOCK, (i + 1) * Q_BLOCK)
    if dilation == 1:
        has_prev = n > 0 if i == 0 else True
        out_rows = rows
    else:
        has_prev = n > 0
        out_rows = pl.ds(rstep * nblk + i, Q_BLOCK, stride=dilation)
    sel = 1 if has_prev is True else jnp.where(has_prev, 1, 0)
    def keys_values(ref_prev, ref_cur, hs):
        if dilation == 1 and i > 0:
            return ref_cur[(i - 1) * Q_BLOCK:(i + 1) * Q_BLOCK, hs]
        prev_rows = slice(0, Q_BLOCK) if dilation == 1 else rows
        return jnp.concatenate([ref_prev[prev_rows, hs], ref_cur[rows, hs]], axis=0)

    def scores(h):
        hs = slice(h * HEAD_DIM, (h + 1) * HEAD_DIM)
        return lax.dot_general(q_ref[rows, hs], keys_values(kp_ref, kc_ref, hs), (((1,), (1,)), ((), ())),
                               preferred_element_type=F32)

    lse = jnp.zeros((Q_BLOCK, HEAD_DIM), F32)
    s_next = scores(0)
    for h in range(HEADS_PER_GROUP):
        s = s_next
        if h + 1 < HEADS_PER_GROUP:
            s_next = scores(h + 1)
        if filler is not None:
            filler(h)
        s = s * scale + bias_ref[sel, h]
        m = jnp.max(s, axis=-1, keepdims=True)
        p = jnp.exp(s - m)
        den = jnp.sum(p, axis=-1, keepdims=True)
        vcat = keys_values(vp_ref, vc_ref, slice(h * HEAD_DIM, (h + 1) * HEAD_DIM))
        o = jnp.dot(p.astype(BF16), vcat, preferred_element_type=F32) / den
        o_ref[h, out_rows, :] = o
        lse = jnp.where(lane == h, m + jnp.log(den), lse)
    l_ref[out_rows, :] = lse


def _attn_body(*refs, dilation, nblk):
    for i in range(nblk):
        _attn_block(i, pl.program_id(1), pl.program_id(2), *refs, dilation=dilation, nblk=nblk)


def _attn_plan(qkv, *, batch, seq, window, dilation, slopes, nblk):
    M, width = qkv.shape
    gw = HEADS_PER_GROUP * HEAD_DIM
    assert width == 3 * gw and seq % (dilation * Q_BLOCK) == 0
    nb = seq // (dilation * Q_BLOCK)
    blocks_per_batch = nb * dilation
    bias = jnp.asarray(_attn_bias(slopes, dilation, window // dilation))
    blk = (nblk * Q_BLOCK, gw)
    if dilation == 1:
        assert nb % nblk == 0
        grid = (batch, nb // nblk, 1)
        first = lambda b, n, r: b * blocks_per_batch + n * nblk
        cur = lambda sec: (lambda b, n, r: (first(b, n, r) // nblk, sec))
        prev = lambda sec: (lambda b, n, r: (jnp.maximum(first(b, n, r) - 1, 0), sec))
        prev_blk = (Q_BLOCK, gw)
        out_rows = nblk * Q_BLOCK
        out_idx = lambda b, n, r: b * (nb // nblk) + n
    else:
        assert dilation % nblk == 0
        grid = (batch, nb, dilation // nblk)
        first = lambda b, n, r: b * blocks_per_batch + n * dilation + r * nblk
        cur = lambda sec: (lambda b, n, r: (first(b, n, r) // nblk, sec))
        prev = lambda sec: (lambda b, n, r: (first(b, jnp.maximum(n - 1, 0), r) // nblk, sec))
        prev_blk = blk
        out_rows = dilation * Q_BLOCK
        out_idx = lambda b, n, r: b * nb + n
    return dict(
        grid=grid,
        operands=(qkv, qkv, qkv, qkv, qkv, bias),
        in_blocks=[blk, blk, prev_blk, blk, prev_blk, bias.shape],
        in_maps=[cur(0), cur(1), prev(1), cur(2), prev(2), lambda b, n, r: (0, 0, 0, 0)],
        out_blocks=[(HEADS_PER_GROUP, out_rows, HEAD_DIM), (out_rows, HEAD_DIM)],
        out_maps=[lambda b, n, r: (0, out_idx(b, n, r), 0), lambda b, n, r: (out_idx(b, n, r), 0)],
        out_shape=[jax.ShapeDtypeStruct((HEADS_PER_GROUP, M, HEAD_DIM), F32), jax.ShapeDtypeStruct((M, HEAD_DIM), F32)],
        vmem=(2 * (3 * nblk * Q_BLOCK + 2 * prev_blk[0]) * gw * 2 + 2 * out_rows * (gw + HEAD_DIM) * 4
              + 2 * bias.size * 4 + (4 << 20)),
    )


def _attention_group(qkv, *, dilation, nblk, **kw):
    plan = _attn_plan(qkv, dilation=dilation, nblk=nblk, **kw)
    return pl.pallas_call(
        functools.partial(_attn_body, dilation=dilation, nblk=nblk),
        grid=plan["grid"],
        in_specs=[pl.BlockSpec(b, m) for b, m in zip(plan["in_blocks"], plan["in_maps"])],
        out_specs=[pl.BlockSpec(b, m) for b, m in zip(plan["out_blocks"], plan["out_maps"])],
        out_shape=plan["out_shape"],
        compiler_params=_params(3, plan["vmem"]),
        name=f"attn_d{dilation}",
    )(*plan["operands"])


def _qkv_attn_body(a_ref, p_ref, w_ref, q_ref, kc_ref, kp_ref, vc_ref, vp_ref, bias_ref, o_ref, ao_ref, al_ref,
                   asub_ref, *, dilation, att_dilation, nblk, steps_per_batch):
    i, j = pl.program_id(0), pl.program_id(1)

    @pl.when(j == 0)
    def _():
        _reorder_rows(a_ref, p_ref, asub_ref, dilation)

    n = (i * pl.num_programs(1) + j) % steps_per_batch
    rc = a_ref.shape[0] // nblk
    ncol = o_ref.shape[1] // V7X_MXU_DIM
    every = -(-HEADS_PER_GROUP // ncol)
    for b in range(nblk):
        rows = slice(b * rc, (b + 1) * rc)

        def project_piece(h, rows=rows):
            if h % every == 0 and h // every < ncol:
                cols = slice(h // every * V7X_MXU_DIM, (h // every + 1) * V7X_MXU_DIM)
                o_ref[rows, cols] = jnp.dot(asub_ref[rows, :], w_ref[:, cols],
                                            preferred_element_type=F32).astype(o_ref.dtype)

        _attn_block(b, n, 0, q_ref, kc_ref, kp_ref, vc_ref, vp_ref, bias_ref, ao_ref, al_ref,
                    dilation=att_dilation, nblk=nblk, filler=project_piece)


def _qkv_attention(a, w, qkv_att, *, tm, tn, dilation, name, att_dilation, nblk, **att_kw):
    M, K = a.shape
    N = w.shape[1]
    nj = N // tn
    assert M % tm == 0 and N % tn == 0 and tm % (dilation * Q_BLOCK) == 0
    plan = _attn_plan(qkv_att, dilation=att_dilation, nblk=nblk, **att_kw)
    gb, gn, gr = plan["grid"]
    assert gr == 1 and (M // tm) * nj == gb * gn

    def hosted(index_map):
        return lambda i, j: index_map((i * nj + j) // gn, (i * nj + j) % gn, 0)

    perm = jnp.asarray(_subseq_perm(dilation), BF16)
    vmem = 2 * (tm * K * 2 + K * tn * 2 + tm * tn * 2) + tm * K * 2 + (tm // nblk) * tn * 4 + plan["vmem"]
    outs = pl.pallas_call(
        functools.partial(_qkv_attn_body, dilation=dilation, att_dilation=att_dilation, nblk=nblk,
                          steps_per_batch=gn),
        grid=(M // tm, nj),
        in_specs=[pl.BlockSpec((tm, K), lambda i, j: (i, 0)), pl.BlockSpec(perm.shape, lambda i, j: (0, 0)),
                  pl.BlockSpec((K, tn), lambda i, j: (0, j))]
                 + [pl.BlockSpec(b, hosted(m)) for b, m in zip(plan["in_blocks"], plan["in_maps"])],
        out_specs=[pl.BlockSpec((tm, tn), lambda i, j: (i, j))]
                  + [pl.BlockSpec(b, hosted(m)) for b, m in zip(plan["out_blocks"], plan["out_maps"])],
        out_shape=[jax.ShapeDtypeStruct((M, N), BF16)] + plan["out_shape"],
        scratch_shapes=[pltpu.VMEM((tm, K), BF16)],
        compiler_params=_params(2, vmem),
        name=name,
    )(a, perm, w, *plan["operands"])
    return outs[0], outs[1:]


def _mix_body(y_ref, o1_ref, l1_ref, o2_ref, l2_ref, o3_ref, l3_ref, gr_ref, ga_ref, wr_ref, wa_ref,
              out_ref, *, row_chunk):
    for c in range(y_ref.shape[0] // row_chunk):
        rows = slice(c * row_chunk, (c + 1) * row_chunk)
        l1, l2, l3 = l1_ref[rows, :], l2_ref[rows, :], l3_ref[rows, :]
        m = jnp.maximum(jnp.maximum(l1, l2), l3)
        e1, e2, e3 = jnp.exp(l1 - m), jnp.exp(l2 - m), jnp.exp(l3 - m)
        den = e1 + e2 + e3
        w1, w2, w3 = e1 / den, e2 / den, e3 / den
        heads = []
        for h in range(HEADS_PER_GROUP):
            bc = lambda w: jnp.broadcast_to(w[:, h:h + 1], (row_chunk, HEAD_DIM))
            att = bc(w1) * o1_ref[h, rows, :] + bc(w2) * o2_ref[h, rows, :] + bc(w3) * o3_ref[h, rows, :]
            heads.append(att.astype(BF16))
        att = jnp.concatenate(heads, axis=1)
        ya = jnp.dot(y_ref[rows, :], wr_ref[...], preferred_element_type=F32)
        yb = jnp.dot(att, wa_ref[...], preferred_element_type=F32)
        out_ref[rows, :] = (gr_ref[rows, :].astype(F32) * ya + ga_ref[rows, :].astype(F32) * yb).astype(out_ref.dtype)


def _mix(y_rec, att_parts, gates, w_rnn, w_att, *, tm, row_chunk):
    M, d_rnn = y_rec.shape
    aw, D = w_att.shape
    rowblk = lambda w: pl.BlockSpec((tm, w), lambda i: (i, 0))
    headblk = pl.BlockSpec((HEADS_PER_GROUP, tm, HEAD_DIM), lambda i: (0, i, 0))
    resident = lambda r: pl.BlockSpec((r, D), lambda i: (0, 0), pipeline_mode=pl.Buffered(1))
    vmem = (2 * (tm * d_rnn * 2 + 3 * tm * (aw + HEAD_DIM) * 4 + 2 * tm * D * 2 + tm * D * 2) + (d_rnn + aw) * D * 2
            + 4 * row_chunk * D * 4)
    return pl.pallas_call(
        functools.partial(_mix_body, row_chunk=row_chunk),
        grid=(M // tm,),
        in_specs=[rowblk(d_rnn)] + [headblk, rowblk(HEAD_DIM)] * 3 + [
            pl.BlockSpec((tm, D), lambda i: (i, 0)),
            pl.BlockSpec((tm, D), lambda i: (i, 1)),
            resident(d_rnn), resident(aw)],
        out_specs=rowblk(D),
        out_shape=jax.ShapeDtypeStruct((M, D), BF16),
        compiler_params=_params(1, vmem),
        name="mix",
    )(y_rec, *att_parts, gates, gates, w_rnn, w_att)


def _outproj_body(m_ref, w_ref, x_ref, g_ref, x1_ref, h2_ref, *, row_chunk):
    for c in range(m_ref.shape[0] // row_chunk):
        rows = slice(c * row_chunk, (c + 1) * row_chunk)
        x1 = x_ref[rows, :] + jnp.dot(m_ref[rows, :], w_ref[...], preferred_element_type=F32)
        x1_ref[rows, :] = x1
        h2_ref[rows, :] = _rms(x1, g_ref[...]).astype(h2_ref.dtype)


def _outproj(mixed, w_out, x, g2, *, tm, row_chunk):
    M, D = x.shape
    vmem = 2 * (tm * D * 2 + D * D * 2 + tm * D * 4 + tm * D * 4 + tm * D * 2) + 2 * tm * D * 4
    return pl.pallas_call(
        functools.partial(_outproj_body, row_chunk=row_chunk),
        grid=(M // tm,),
        in_specs=[pl.BlockSpec((tm, D), lambda i: (i, 0)), pl.BlockSpec((D, D), lambda i: (0, 0)),
                  pl.BlockSpec((tm, D), lambda i: (i, 0)), pl.BlockSpec((1, D), lambda i: (0, 0))],
        out_specs=[pl.BlockSpec((tm, D), lambda i: (i, 0))] * 2,
        out_shape=[jax.ShapeDtypeStruct((M, D), F32), jax.ShapeDtypeStruct((M, D), BF16)],
        compiler_params=_params(1, vmem),
        name="outproj",
    )(mixed, w_out, x, g2)


def _ffn_up_body(h_ref, wg_ref, wv_ref, cw_ref, cb_ref, o_ref, gs_ref, vs_ref, halo_ref, *, tiles_per_batch,
                 row_chunk):
    i = pl.program_id(0)
    j = pl.program_id(1)
    tm, tn = o_ref.shape

    @pl.when(i % tiles_per_batch == 0)
    def _():
        gs_ref[0:V7X_SUBLANES, :] = jnp.zeros((V7X_SUBLANES, tn), F32)

    @pl.when(i % tiles_per_batch != 0)
    def _():
        gs_ref[0:V7X_SUBLANES, :] = halo_ref[j]

    for c in range(tm // row_chunk):
        rows = slice(c * row_chunk, (c + 1) * row_chunk)
        base = V7X_SUBLANES + c * row_chunk
        h = h_ref[rows, :]
        gs_ref[base:base + row_chunk, :] = jnp.dot(h, wg_ref[...], preferred_element_type=F32)
        vs_ref[rows, :] = jnp.dot(h, wv_ref[...], preferred_element_type=F32)
        for s in range(tn // V7X_LANES):
            cs = slice(s * V7X_LANES, (s + 1) * V7X_LANES)
            gate = cb_ref[:, cs] + cw_ref[CONV_FFN - 1:CONV_FFN, cs] * gs_ref[base:base + row_chunk, cs]
            for t in range(CONV_FFN - 1):
                r0 = base - (CONV_FFN - 1) + t
                gate = gate + cw_ref[t:t + 1, cs] * gs_ref[r0:r0 + row_chunk, cs]
            o_ref[rows, cs] = (jax.nn.gelu(gate) * vs_ref[rows, cs]).astype(o_ref.dtype)
    halo_ref[j] = gs_ref[tm:tm + V7X_SUBLANES, :]


def _ffn_up(h2, w_up, conv_w, conv_b, *, seq, tm, tn, row_chunk):
    M, D = h2.shape
    d_ff = w_up.shape[1] // 2
    nj = d_ff // tn
    vmem = (2 * (tm * D * 2 + 2 * D * tn * 2 + tm * tn * 2) + (tm + 8) * tn * 4 + nj * 8 * tn * 4
            + 6 * row_chunk * tn * 4)
    return pl.pallas_call(
        functools.partial(_ffn_up_body, tiles_per_batch=seq // tm, row_chunk=row_chunk),
        grid=(M // tm, nj),
        in_specs=[pl.BlockSpec((tm, D), lambda i, j: (i, 0)),
                  pl.BlockSpec((D, tn), lambda i, j: (0, j)),
                  pl.BlockSpec((D, tn), lambda i, j: (0, j + nj)),
                  pl.BlockSpec((CONV_FFN, tn), lambda i, j: (0, j)),
                  pl.BlockSpec((1, tn), lambda i, j: (0, j))],
        out_specs=pl.BlockSpec((tm, tn), lambda i, j: (i, j)),
        out_shape=jax.ShapeDtypeStruct((M, d_ff), BF16),
        scratch_shapes=[pltpu.VMEM((tm + V7X_SUBLANES, tn), F32),
                        pltpu.VMEM((tm, tn), F32),
                        pltpu.VMEM((nj, V7X_SUBLANES, tn), F32)],
        compiler_params=_params(2, vmem),
        name="ffn_up",
    )(h2, w_up, w_up, conv_w, conv_b)


def _ffn_down_body(a_ref, w_ref, x_ref, g_ref, o_ref, *, final_norm, row_chunk):
    for c in range(a_ref.shape[0] // row_chunk):
        rows = slice(c * row_chunk, (c + 1) * row_chunk)
        x2 = x_ref[rows, :] + jnp.dot(a_ref[rows, :], w_ref[...], preferred_element_type=F32)
        o_ref[rows, :] = _rms(x2, g_ref[...]) if final_norm else x2


def _ffn_down(a, w_down, x1, g, *, tm, row_chunk, final_norm):
    M, d_ff = a.shape
    D = w_down.shape[1]
    vmem = 2 * (tm * d_ff * 2 + 2 * tm * D * 4) + d_ff * D * 2 + 2 * tm * D * 4
    return pl.pallas_call(
        functools.partial(_ffn_down_body, final_norm=final_norm, row_chunk=row_chunk),
        grid=(M // tm,),
        in_specs=[pl.BlockSpec((tm, d_ff), lambda i: (i, 0)),
                  pl.BlockSpec((d_ff, D), lambda i: (0, 0), pipeline_mode=pl.Buffered(1)),
                  pl.BlockSpec((tm, D), lambda i: (i, 0)),
                  pl.BlockSpec((1, D), lambda i: (0, 0))],
        out_specs=pl.BlockSpec((tm, D), lambda i: (i, 0)),
        out_shape=jax.ShapeDtypeStruct((M, D), F32),
        compiler_params=_params(1, vmem),
        name="ffn_down",
    )(a, w_down, x1, g)


def _layer(x2d, p, *, batch, seq):
    d_rnn = p["conv_w"].shape[1]
    gw = HEADS_PER_GROUP * HEAD_DIM
    att_w = len(ATT_GROUPS) * gw
    off_q = 2 * d_rnn
    off_g = off_q + 3 * att_w
    w_in = p["w_in"].astype(BF16)

    h, gates = _norm_gates(x2d, p["norm1_g"][None, :], w_in[:, off_g:], tm=1024, tn=1024, row_chunk=512)

    band = _lru_band(d_rnn, d_rnn // LRU_BLOCKS)
    y_rec = _lru(h, w_in[:, :off_q], p["conv_w"], p["conv_b"][None, :], _band_tiles(p["lru_wr"], band),
                 _band_tiles(p["lru_wi"], band), p["lru_br"][None, :], p["lru_bi"][None, :], p["lru_lambda"][None, :],
                 seq=seq, ts=256, tiles=2, band=band)

    slopes = _alibi_slopes(len(ATT_GROUPS) * HEADS_PER_GROUP).reshape(len(ATT_GROUPS), HEADS_PER_GROUP)
    att_parts, qkv_prev, att_prev = [], None, None
    for g, (window, dilation) in enumerate(ATT_GROUPS):
        w_g = jnp.concatenate([w_in[:, off_q + sec * att_w + g * gw: off_q + sec * att_w + (g + 1) * gw]
                               for sec in range(3)], axis=1)
        if dilation == 1:
            qkv = _matmul(h, w_g, tm=2048, tn=1024, name=f"inproj_qkv_d{dilation}")
        else:
            qkv, parts = _qkv_attention(h, w_g, qkv_prev, tm=2048, tn=768, dilation=dilation,
                                        name=f"inproj_qkv_d{dilation}_attn_d{att_prev['dilation']}", nblk=4,
                                        att_dilation=att_prev["dilation"], batch=batch, seq=seq,
                                        window=att_prev["window"], slopes=att_prev["slopes"])
            att_parts += parts
        qkv_prev, att_prev = qkv, dict(window=window, dilation=dilation, slopes=slopes[g])
    att_parts += _attention_group(qkv_prev, batch=batch, seq=seq, nblk=4, **att_prev)

    mixed = _mix(y_rec, att_parts, gates, p["w_rnn_out"].astype(BF16), p["w_att_out"].astype(BF16),
                 tm=512, row_chunk=512)
    x1, h2 = _outproj(mixed, p["w_out"].astype(BF16), x2d, p["norm2_g"][None, :], tm=512, row_chunk=512)
    hmid = _ffn_up(h2, p["w_up"].astype(BF16), p["ffn_conv_w"], p["ffn_conv_b"][None, :], seq=seq,
                   tm=2048, tn=512, row_chunk=1024)
    return x1, hmid


def kernel(x, norm1_g, w_in, conv_w, conv_b, lru_wr, lru_br, lru_wi, lru_bi, lru_lambda, w_rnn_out, w_att_out, w_out,
           norm2_g, w_up, ffn_conv_w, ffn_conv_b, w_down, final_g):
    B, S, D = x.shape
    depth = w_in.shape[0]
    x2d = x.reshape(B * S, D)
    for l in range(depth):
        p = dict(norm1_g=norm1_g[l], w_in=w_in[l], conv_w=conv_w[l], conv_b=conv_b[l], lru_wr=lru_wr[l],
                 lru_br=lru_br[l], lru_wi=lru_wi[l], lru_bi=lru_bi[l], lru_lambda=lru_lambda[l],
                 w_rnn_out=w_rnn_out[l], w_att_out=w_att_out[l], w_out=w_out[l], norm2_g=norm2_g[l],
                 w_up=w_up[l], ffn_conv_w=ffn_conv_w[l], ffn_conv_b=ffn_conv_b[l])
        x1, hmid = _layer(x2d, p, batch=B, seq=S)
        x2d = _ffn_down(hmid, w_down[l].astype(BF16), x1, final_g[None, :], tm=512, row_chunk=512,
                        final_norm=l == depth - 1)
    return x2d.reshape(B, S, D)
```

```python
import functools
import math

import numpy as np
import jax
import jax.numpy as jnp
from jax import lax
from jax.experimental import pallas as pl
from jax.experimental.pallas import tpu as pltpu

LRU_BLOCKS = 16
LRU_C = 8.0
CONV_REC = 4
ATT_GROUPS = ((128, 1), (512, 4), (2048, 16))
HEADS_PER_GROUP = 8
HEAD_DIM = 128
Q_BLOCK = 128
CONV_FFN = 3
EPS = 1e-6

V7X_LANES = 128
V7X_SUBLANES = 8
V7X_MXU_DIM = 256
V7X_VMEM_BYTES = 64 * 1024 * 1024

F32 = jnp.float32
BF16 = jnp.bfloat16


def _params(n_axes, vmem_bytes):
    limit = min(int(vmem_bytes * 1.25) + (4 << 20), V7X_VMEM_BYTES - (4 << 20))
    return pltpu.CompilerParams(dimension_semantics=("arbitrary",) * n_axes, vmem_limit_bytes=limit)


def _rms(xf, g):
    return xf * lax.rsqrt(jnp.mean(xf * xf, axis=-1, keepdims=True) + EPS) * g


def _norm_body(x_ref, g_ref, o_ref):
    o_ref[...] = _rms(x_ref[...], g_ref[...]).astype(o_ref.dtype)


def _norm(x, g, *, tm):
    M, D = x.shape
    return pl.pallas_call(
        _norm_body,
        grid=(M // tm,),
        in_specs=[pl.BlockSpec((tm, D), lambda i: (i, 0)), pl.BlockSpec((1, D), lambda i: (0, 0))],
        out_specs=pl.BlockSpec((tm, D), lambda i: (i, 0)),
        out_shape=jax.ShapeDtypeStruct((M, D), BF16),
        compiler_params=_params(1, 2 * tm * D * 6),
        name="norm1",
    )(x, g)


def _mm_body(a_ref, w_ref, o_ref):
    o_ref[...] = jnp.dot(a_ref[...], w_ref[...], preferred_element_type=F32).astype(o_ref.dtype)


BF16_TILE_ROWS = 16


def _subseq_perm(dilation):
    lb = BF16_TILE_ROWS * dilation
    assert V7X_MXU_DIM % lb == 0
    p = np.arange(V7X_MXU_DIM)
    q = p % lb
    perm = np.zeros((V7X_MXU_DIM, V7X_MXU_DIM), np.float32)
    perm[p, p // lb * lb + (q % BF16_TILE_ROWS) * dilation + q // BF16_TILE_ROWS] = 1.0
    return perm


def _reorder_rows(a_ref, p_ref, asub_ref, dilation):
    tm = a_ref.shape[0]
    lb = BF16_TILE_ROWS * dilation
    chunk = dilation * Q_BLOCK
    for blk in range(tm // V7X_MXU_DIM):
        t = jnp.dot(p_ref[...], a_ref[blk * V7X_MXU_DIM:(blk + 1) * V7X_MXU_DIM, :],
                    preferred_element_type=F32).astype(BF16)
        for l in range(V7X_MXU_DIM // lb):
            g = blk * (V7X_MXU_DIM // lb) + l
            base = (g // V7X_SUBLANES) * chunk + (g % V7X_SUBLANES) * BF16_TILE_ROWS
            for r in range(dilation):
                src = l * lb + r * BF16_TILE_ROWS
                dst = base + r * Q_BLOCK
                asub_ref[dst:dst + BF16_TILE_ROWS, :] = t[src:src + BF16_TILE_ROWS, :]


def _matmul(a, w, *, tm, tn, name):
    M, K = a.shape
    N = w.shape[1]
    assert M % tm == 0 and N % tn == 0
    return pl.pallas_call(
        _mm_body,
        grid=(M // tm, N // tn),
        in_specs=[pl.BlockSpec((tm, K), lambda i, j: (i, 0)), pl.BlockSpec((K, tn), lambda i, j: (0, j))],
        out_specs=pl.BlockSpec((tm, tn), lambda i, j: (i, j)),
        out_shape=jax.ShapeDtypeStruct((M, N), BF16),
        compiler_params=_params(2, 2 * (tm * K * 2 + K * tn * 2 + tm * tn * 2) + tm * tn * 4),
        name=name,
    )(a, w)


def _lru_band(d_rnn, block):
    band = []
    for c0 in range(0, d_rnn, V7X_MXU_DIM):
        cn = min(V7X_MXU_DIM, d_rnn - c0)
        b0, b1 = c0 // block, (c0 + cn - 1) // block
        r0 = (block * b0) // V7X_LANES * V7X_LANES
        r1 = min(-(-(block * (b1 + 1)) // V7X_LANES) * V7X_LANES, d_rnn)
        band.append((c0, cn, r0, r1 - r0))
    return tuple(band)


def _band_tiles(w, band):
    dense = jax.scipy.linalg.block_diag(*[w[n] for n in range(w.shape[0])])
    kmax = max(k for _, _, _, k in band)
    tiles = []
    for c0, cn, r0, k in band:
        t = dense[r0:r0 + k, c0:c0 + cn]
        tiles.append(jnp.pad(t, ((0, kmax - k), (0, V7X_MXU_DIM - cn))))
    return jnp.stack(tiles).astype(BF16)


def _segment_order(ts):
    p = np.arange(ts)
    perm = np.zeros((ts, ts), np.float32)
    perm[p, (p % V7X_SUBLANES) * (ts // V7X_SUBLANES) + p // V7X_SUBLANES] = 1.0
    return perm


def _shift_sublanes(v, s, fill):
    row = lax.broadcasted_iota(jnp.int32, v.shape, 0)
    return jnp.where(row < s, fill, pltpu.roll(v, s, axis=0))


_LRU_HALO = (CONV_REC - 1) * V7X_SUBLANES


def _lru_body(h_ref, pin_ref, pout_ref, w_ref, cw_ref, cb_ref, wr_ref, wi_ref, br_ref, bi_ref, lam_ref, y_ref,
              hil_ref, xs_ref, xf_ref, xb_ref, gs_ref, tail_ref, hc_ref, *, ts, steps_per_batch, band):
    @pl.when(pl.program_id(0) % steps_per_batch == 0)
    def _():
        tail_ref[...] = jnp.zeros(tail_ref.shape, F32)
        hc_ref[...] = jnp.zeros(hc_ref.shape, F32)

    for t in range(h_ref.shape[0] // ts):
        rows = slice(t * ts, (t + 1) * ts)
        _lru_tile(h_ref.at[rows], pin_ref, pout_ref, w_ref, cw_ref, cb_ref, wr_ref, wi_ref, br_ref, bi_ref, lam_ref,
                  y_ref.at[rows], hil_ref.at[t], xs_ref.at[t], xf_ref.at[t], xb_ref.at[t], gs_ref.at[t],
                  tail_ref, hc_ref, ts=ts, band=band)


def _lru_tile(h_ref, pin_ref, pout_ref, w_ref, cw_ref, cb_ref, wr_ref, wi_ref, br_ref, bi_ref, lam_ref, y_ref,
              hil_ref, xs_ref, xf_ref, xb_ref, gs_ref, tail_ref, hc_ref, *, ts, band):
    d_rnn = y_ref.shape[1]
    nseg = V7X_SUBLANES
    seg_len = ts // nseg

    hil_ref[...] = jnp.dot(pin_ref[...], h_ref[...], preferred_element_type=F32).astype(BF16)

    row8 = lax.broadcasted_iota(jnp.int32, (V7X_SUBLANES, V7X_MXU_DIM), 0)

    def project(c0, cn):
        cs = slice(c0, c0 + cn)
        xs_ref[_LRU_HALO:_LRU_HALO + ts, cs] = jnp.dot(hil_ref[...], w_ref[:, cs], preferred_element_type=F32)
        gs_ref[:, cs] = jnp.dot(hil_ref[...], w_ref[:, d_rnn + c0:d_rnn + c0 + cn], preferred_element_type=F32)
        for m in range(CONV_REC - 1):
            r0 = ts + m * V7X_SUBLANES
            cur8 = pltpu.roll(xs_ref[r0:r0 + V7X_SUBLANES, cs], 1, axis=0)
            prev8 = pltpu.roll(tail_ref[m * V7X_SUBLANES:(m + 1) * V7X_SUBLANES, cs], 1, axis=0)
            xs_ref[m * V7X_SUBLANES:(m + 1) * V7X_SUBLANES, cs] = jnp.where(row8[:, :cn] == 0, prev8, cur8)
        tail_ref[:, cs] = xs_ref[ts:ts + _LRU_HALO, cs]
        acc = cb_ref[:, cs] + cw_ref[CONV_REC - 1:CONV_REC, cs] * xs_ref[_LRU_HALO:_LRU_HALO + ts, cs]
        for i in range(CONV_REC - 1):
            r0 = i * V7X_SUBLANES
            acc = acc + cw_ref[i:i + 1, cs] * xs_ref[r0:r0 + ts, cs]
        xf_ref[:, cs] = acc
        xb_ref[:, cs] = acc.astype(BF16)

    def emit(cols, y_il):
        y_ref[:, cols] = jnp.dot(pout_ref[...], y_il, preferred_element_type=F32).astype(y_ref.dtype)

    projected = 0
    pending = None
    for j, (c0, cn, r0, k) in enumerate(band):
        ahead = band[min(j + 1, len(band) - 1)]
        while projected < len(band) and band[projected][0] < ahead[2] + ahead[3]:
            project(*band[projected][:2])
            projected += 1
        cs = slice(c0, c0 + cn)
        win = xb_ref[:, r0:r0 + k]
        zr = jnp.dot(win, wr_ref[j, :k, :], preferred_element_type=F32)[:, :cn] + br_ref[:, cs]
        zi = jnp.dot(win, wi_ref[j, :k, :], preferred_element_type=F32)[:, :cn] + bi_ref[:, cs]
        if pending is not None:
            emit(*pending)
        r = jax.nn.sigmoid(zr)
        gi = jax.nn.sigmoid(zi)
        log_a = -LRU_C * r * jax.nn.softplus(-lam_ref[:, cs])
        a = jnp.exp(log_a)
        b = jnp.sqrt(1.0 - jnp.exp(2.0 * log_a)) * (gi * xf_ref[:, cs])
        v = lambda x, i: x[i * V7X_SUBLANES:(i + 1) * V7X_SUBLANES]
        h_loc, a_cum = [v(b, 0)], [v(a, 0)]
        for i in range(1, seg_len):
            h_loc.append(v(a, i) * h_loc[-1] + v(b, i))
            a_cum.append(v(a, i) * a_cum[-1])
        a_inc, h_inc = a_cum[-1], h_loc[-1]
        s = 1
        while s < nseg:
            h_inc = a_inc * _shift_sublanes(h_inc, s, 0.0) + h_inc
            a_inc = a_inc * _shift_sublanes(a_inc, s, 1.0)
            s *= 2
        c_in = hc_ref[0:1, cs]
        c_end = a_inc * c_in + h_inc
        c_seg = _shift_sublanes(c_end, 1, c_in)
        hc_ref[0:1, cs] = c_end[nseg - 1:nseg, :]
        h = jnp.concatenate([a_cum[i] * c_seg + h_loc[i] for i in range(seg_len)], axis=0)
        pending = (cs, (h * jax.nn.gelu(gs_ref[:, cs])).astype(BF16))
    emit(*pending)


def _lru(h, w_rec, conv_w, conv_b, wr_t, wi_t, br, bi, lam, *, seq, ts, tiles, band):
    M, D = h.shape
    d_rnn = w_rec.shape[1] // 2
    nt, kmax, _ = wr_t.shape
    perm = _segment_order(ts)
    rows = tiles * ts
    row = lambda i: (0, 0)
    resident = lambda shape: pl.BlockSpec(shape, lambda i: (0,) * len(shape), pipeline_mode=pl.Buffered(1))
    vmem = (D * 2 * d_rnn * 2 + 2 * nt * kmax * V7X_MXU_DIM * 2 + 2 * rows * (D + d_rnn) * 2
            + tiles * (ts * D * 2 + (ts + _LRU_HALO) * d_rnn * 4 + ts * d_rnn * 10) + 24 * ts * V7X_MXU_DIM * 4)
    return pl.pallas_call(
        functools.partial(_lru_body, ts=ts, steps_per_batch=seq // rows, band=band),
        grid=(M // rows,),
        in_specs=[
            pl.BlockSpec((rows, D), lambda i: (i, 0)),
            pl.BlockSpec((ts, ts), row),
            pl.BlockSpec((ts, ts), row),
            resident((D, 2 * d_rnn)),
            pl.BlockSpec((CONV_REC, d_rnn), row),
            pl.BlockSpec((1, d_rnn), row),
            resident((nt, kmax, V7X_MXU_DIM)),
            resident((nt, kmax, V7X_MXU_DIM)),
            pl.BlockSpec((1, d_rnn), row),
            pl.BlockSpec((1, d_rnn), row),
            pl.BlockSpec((1, d_rnn), row),
        ],
        out_specs=pl.BlockSpec((rows, d_rnn), lambda i: (i, 0)),
        out_shape=jax.ShapeDtypeStruct((M, d_rnn), BF16),
        scratch_shapes=[
            pltpu.VMEM((tiles, ts, D), BF16),
            pltpu.VMEM((tiles, ts + _LRU_HALO, d_rnn), F32),
            pltpu.VMEM((tiles, ts, d_rnn), F32),
            pltpu.VMEM((tiles, ts, d_rnn), BF16),
            pltpu.VMEM((tiles, ts, d_rnn), F32),
            pltpu.VMEM((_LRU_HALO, d_rnn), F32),
            pltpu.VMEM((V7X_SUBLANES, d_rnn), F32),
        ],
        compiler_params=_params(1, vmem),
        name="rglru",
    )(h, jnp.asarray(perm, BF16), jnp.asarray(perm.T, BF16), w_rec, conv_w, conv_b, wr_t, wi_t, br, bi, lam)


def _alibi_slopes(n):
    def pow2_slopes(m):
        start = 2.0 ** (-8.0 / m)
        return [start ** (i + 1) for i in range(m)]
    c = 2 ** int(math.floor(math.log2(n)))
    s = pow2_slopes(c) + pow2_slopes(2 * c)[0::2][: n - c]
    return np.sort(np.asarray(s, np.float32))[::-1].copy()


def _attn_bias(slopes, dilation, span):
    qi = np.arange(Q_BLOCK)[:, None]
    ki = np.arange(2 * Q_BLOCK)[None, :]
    steps = qi + Q_BLOCK - ki
    band = (steps >= 0) & (steps <= span)
    bias = -np.asarray(slopes, np.float32)[:, None, None] * (steps * dilation).astype(np.float32)[None]
    with_prev = np.where(band[None], bias, -np.inf)
    no_prev = np.where((band & (ki >= Q_BLOCK))[None], bias, -np.inf)
    return np.stack([no_prev, with_prev]).astype(np.float32)


def _attn_block(i, n, rstep, q_ref, kc_ref, kp_ref, vc_ref, vp_ref, bias_ref, o_ref, l_ref, *, dilation, nblk,
                filler=None):
    scale = HEAD_DIM ** -0.5
    lane = lax.broadcasted_iota(jnp.int32, (Q_BLOCK, HEAD_DIM), 1)
    rows = slice(i * Q_BLOCK, (i + 1) * Q_BLOCK)
    if dilation == 1:
        has_prev = n > 0 if i == 0 else True
        out_rows = rows
    else:
        has_prev = n > 0
        out_rows = pl.ds(rstep * nblk + i, Q_BLOCK, stride=dilation)
    sel = 1 if has_prev is True else jnp.where(has_prev, 1, 0)
    def keys_values(ref_prev, ref_cur, hs):
        if dilation == 1 and i > 0:
            return ref_cur[(i - 1) * Q_BLOCK:(i + 1) * Q_BLOCK, hs]
        prev_rows = slice(0, Q_BLOCK) if dilation == 1 else rows
        return jnp.concatenate([ref_prev[prev_rows, hs], ref_cur[rows, hs]], axis=0)

    def scores(h):
        hs = slice(h * HEAD_DIM, (h + 1) * HEAD_DIM)
        return lax.dot_general(q_ref[rows, hs], keys_values(kp_ref, kc_ref, hs), (((1,), (1,)), ((), ())),
                               preferred_element_type=F32)

    lse = jnp.zeros((Q_BLOCK, HEAD_DIM), F32)
    s_next = scores(0)
    for h in range(HEADS_PER_GROUP):
        s = s_next
        if h + 1 < HEADS_PER_GROUP:
            s_next = scores(h + 1)
        if filler is not None:
            filler(h)
        s = s * scale + bias_ref[sel, h]
        m = jnp.max(s, axis=-1, keepdims=True)
        p = jnp.exp(s - m)
        den = jnp.sum(p, axis=-1, keepdims=True)
        vcat = keys_values(vp_ref, vc_ref, slice(h * HEAD_DIM, (h + 1) * HEAD_DIM))
        o = jnp.dot(p.astype(BF16), vcat, preferred_element_type=F32) / den
        o_ref[h, out_rows, :] = o
        lse = jnp.where(lane == h, m + jnp.log(den), lse)
    l_ref[out_rows, :] = lse


def _attn_plan(qkv, *, batch, seq, window, dilation, slopes, nblk):
    M, width = qkv.shape
    gw = HEADS_PER_GROUP * HEAD_DIM
    assert width == 3 * gw and seq % (dilation * Q_BLOCK) == 0
    nb = seq // (dilation * Q_BLOCK)
    blocks_per_batch = nb * dilation
    bias = jnp.asarray(_attn_bias(slopes, dilation, window // dilation))
    blk = (nblk * Q_BLOCK, gw)
    if dilation == 1:
        assert nb % nblk == 0
        grid = (batch, nb // nblk, 1)
        first = lambda b, n, r: b * blocks_per_batch + n * nblk
        cur = lambda sec: (lambda b, n, r: (first(b, n, r) // nblk, sec))
        prev = lambda sec: (lambda b, n, r: (jnp.maximum(first(b, n, r) - 1, 0), sec))
        prev_blk = (Q_BLOCK, gw)
        out_rows = nblk * Q_BLOCK
        out_idx = lambda b, n, r: b * (nb // nblk) + n
    else:
        assert dilation % nblk == 0
        grid = (batch, nb, dilation // nblk)
        first = lambda b, n, r: b * blocks_per_batch + n * dilation + r * nblk
        cur = lambda sec: (lambda b, n, r: (first(b, n, r) // nblk, sec))
        prev = lambda sec: (lambda b, n, r: (first(b, jnp.maximum(n - 1, 0), r) // nblk, sec))
        prev_blk = blk
        out_rows = dilation * Q_BLOCK
        out_idx = lambda b, n, r: b * nb + n
    return dict(
        grid=grid,
        operands=(qkv, qkv, qkv, qkv, qkv, bias),
        in_blocks=[blk, blk, prev_blk, blk, prev_blk, bias.shape],
        in_maps=[cur(0), cur(1), prev(1), cur(2), prev(2), lambda b, n, r: (0, 0, 0, 0)],
        out_blocks=[(HEADS_PER_GROUP, out_rows, HEAD_DIM), (out_rows, HEAD_DIM)],
        out_maps=[lambda b, n, r: (0, out_idx(b, n, r), 0), lambda b, n, r: (out_idx(b, n, r), 0)],
        out_shape=[jax.ShapeDtypeStruct((HEADS_PER_GROUP, M, HEAD_DIM), F32), jax.ShapeDtypeStruct((M, HEAD_DIM), F32)],
        vmem=(2 * (3 * nblk * Q_BLOCK + 2 * prev_blk[0]) * gw * 2 + 2 * out_rows * (gw + HEAD_DIM) * 4
              + 2 * bias.size * 4 + (4 << 20)),
    )


def _sigmoid(x):
    return 0.5 * (jnp.tanh(0.5 * x) + 1.0)


def _host_body(*refs, dilation, act, att_dilation, nblk, att_grid):
    if dilation > 1:
        a_ref, p_ref, w_ref, *att_in, o_ref, ao_ref, al_ref, asub_ref = refs
    else:
        a_ref, w_ref, *att_in, o_ref, ao_ref, al_ref = refs
        asub_ref = a_ref
    i, j = pl.program_id(0), pl.program_id(1)

    if dilation > 1:
        @pl.when(j == 0)
        def _():
            _reorder_rows(a_ref, p_ref, asub_ref, dilation)

    _, gn, gr = att_grid
    step = i * pl.num_programs(1) + j
    n, rstep = (step // gr) % gn, step % gr
    rc = a_ref.shape[0] // nblk
    ncol = o_ref.shape[1] // V7X_MXU_DIM
    every = -(-HEADS_PER_GROUP // ncol)
    for b in range(nblk):
        rows = slice(b * rc, (b + 1) * rc)

        def project_piece(h, rows=rows):
            if h % every == 0 and h // every < ncol:
                cols = slice(h // every * V7X_MXU_DIM, (h // every + 1) * V7X_MXU_DIM)
                acc = jnp.dot(asub_ref[rows, :], w_ref[:, cols], preferred_element_type=F32)
                o_ref[rows, cols] = (_sigmoid(acc) if act == "sigmoid" else acc).astype(o_ref.dtype)

        _attn_block(b, n, rstep, *att_in, ao_ref, al_ref, dilation=att_dilation, nblk=nblk, filler=project_piece)


def _project_hosting_attention(a, w, qkv_att, *, tm, tn, name, att_dilation, nblk, dilation=1, act=None, **att_kw):
    M, K = a.shape
    N = w.shape[1]
    nj = N // tn
    assert M % tm == 0 and N % tn == 0 and tm % (dilation * Q_BLOCK) == 0 and tn // V7X_MXU_DIM <= HEADS_PER_GROUP
    plan = _attn_plan(qkv_att, dilation=att_dilation, nblk=nblk, **att_kw)
    gb, gn, gr = plan["grid"]
    assert (M // tm) * nj == gb * gn * gr

    def hosted(index_map):
        def host_map(i, j):
            step = i * nj + j
            return index_map(step // (gn * gr), (step // gr) % gn, step % gr)
        return host_map

    in_specs = [pl.BlockSpec((tm, K), lambda i, j: (i, 0))]
    operands = [a]
    scratch = []
    vmem = 2 * (tm * K * 2 + K * tn * 2 + tm * tn * 2) + (tm // nblk) * tn * 4 + plan["vmem"]
    if dilation > 1:
        perm = jnp.asarray(_subseq_perm(dilation), BF16)
        in_specs.append(pl.BlockSpec(perm.shape, lambda i, j: (0, 0)))
        operands.append(perm)
        scratch = [pltpu.VMEM((tm, K), BF16)]
        vmem += tm * K * 2
    in_specs.append(pl.BlockSpec((K, tn), lambda i, j: (0, j)))
    outs = pl.pallas_call(
        functools.partial(_host_body, dilation=dilation, act=act, att_dilation=att_dilation, nblk=nblk,
                          att_grid=plan["grid"]),
        grid=(M // tm, nj),
        in_specs=in_specs + [pl.BlockSpec(b, hosted(m)) for b, m in zip(plan["in_blocks"], plan["in_maps"])],
        out_specs=[pl.BlockSpec((tm, tn), lambda i, j: (i, j))]
                  + [pl.BlockSpec(b, hosted(m)) for b, m in zip(plan["out_blocks"], plan["out_maps"])],
        out_shape=[jax.ShapeDtypeStruct((M, N), BF16)] + plan["out_shape"],
        scratch_shapes=scratch,
        compiler_params=_params(2, vmem),
        name=name,
    )(*operands, w, *plan["operands"])
    return outs[0], outs[1:]


def _mix_body(y_ref, o1_ref, l1_ref, o2_ref, l2_ref, o3_ref, l3_ref, gr_ref, ga_ref, wr_ref, wa_ref,
              out_ref, *, row_chunk):
    for c in range(y_ref.shape[0] // row_chunk):
        rows = slice(c * row_chunk, (c + 1) * row_chunk)
        l1, l2, l3 = l1_ref[rows, :], l2_ref[rows, :], l3_ref[rows, :]
        m = jnp.maximum(jnp.maximum(l1, l2), l3)
        e1, e2, e3 = jnp.exp(l1 - m), jnp.exp(l2 - m), jnp.exp(l3 - m)
        den = e1 + e2 + e3
        w1, w2, w3 = e1 / den, e2 / den, e3 / den
        heads = []
        for h in range(HEADS_PER_GROUP):
            bc = lambda w: jnp.broadcast_to(w[:, h:h + 1], (row_chunk, HEAD_DIM))
            att = bc(w1) * o1_ref[h, rows, :] + bc(w2) * o2_ref[h, rows, :] + bc(w3) * o3_ref[h, rows, :]
            heads.append(att.astype(BF16))
        att = jnp.concatenate(heads, axis=1)
        ya = jnp.dot(y_ref[rows, :], wr_ref[...], preferred_element_type=F32)
        yb = jnp.dot(att, wa_ref[...], preferred_element_type=F32)
        out_ref[rows, :] = (gr_ref[rows, :].astype(F32) * ya + ga_ref[rows, :].astype(F32) * yb).astype(out_ref.dtype)


def _mix(y_rec, att_parts, gates, w_rnn, w_att, *, tm, row_chunk):
    M, d_rnn = y_rec.shape
    aw, D = w_att.shape
    rowblk = lambda w: pl.BlockSpec((tm, w), lambda i: (i, 0))
    headblk = pl.BlockSpec((HEADS_PER_GROUP, tm, HEAD_DIM), lambda i: (0, i, 0))
    resident = lambda r: pl.BlockSpec((r, D), lambda i: (0, 0), pipeline_mode=pl.Buffered(1))
    vmem = (2 * (tm * d_rnn * 2 + 3 * tm * (aw + HEAD_DIM) * 4 + 2 * tm * D * 2 + tm * D * 2) + (d_rnn + aw) * D * 2
            + 4 * row_chunk * D * 4)
    return pl.pallas_call(
        functools.partial(_mix_body, row_chunk=row_chunk),
        grid=(M // tm,),
        in_specs=[rowblk(d_rnn)] + [headblk, rowblk(HEAD_DIM)] * 3 + [
            pl.BlockSpec((tm, D), lambda i: (i, 0)),
            pl.BlockSpec((tm, D), lambda i: (i, 1)),
            resident(d_rnn), resident(aw)],
        out_specs=rowblk(D),
        out_shape=jax.ShapeDtypeStruct((M, D), BF16),
        compiler_params=_params(1, vmem),
        name="mix",
    )(y_rec, *att_parts, gates, gates, w_rnn, w_att)


def _outproj_body(m_ref, w_ref, x_ref, g_ref, x1_ref, h2_ref, *, row_chunk):
    for c in range(m_ref.shape[0] // row_chunk):
        rows = slice(c * row_chunk, (c + 1) * row_chunk)
        x1 = x_ref[rows, :] + jnp.dot(m_ref[rows, :], w_ref[...], preferred_element_type=F32)
        x1_ref[rows, :] = x1
        h2_ref[rows, :] = _rms(x1, g_ref[...]).astype(h2_ref.dtype)


def _outproj(mixed, w_out, x, g2, *, tm, row_chunk):
    M, D = x.shape
    vmem = 2 * (tm * D * 2 + D * D * 2 + tm * D * 4 + tm * D * 4 + tm * D * 2) + 2 * tm * D * 4
    return pl.pallas_call(
        functools.partial(_outproj_body, row_chunk=row_chunk),
        grid=(M // tm,),
        in_specs=[pl.BlockSpec((tm, D), lambda i: (i, 0)), pl.BlockSpec((D, D), lambda i: (0, 0)),
                  pl.BlockSpec((tm, D), lambda i: (i, 0)), pl.BlockSpec((1, D), lambda i: (0, 0))],
        out_specs=[pl.BlockSpec((tm, D), lambda i: (i, 0))] * 2,
        out_shape=[jax.ShapeDtypeStruct((M, D), F32), jax.ShapeDtypeStruct((M, D), BF16)],
        compiler_params=_params(1, vmem),
        name="outproj",
    )(mixed, w_out, x, g2)


def _ffn_up_body(h_ref, wg_ref, wv_ref, cw_ref, cb_ref, o_ref, gs_ref, vs_ref, halo_ref, *, tiles_per_batch,
                 row_chunk):
    i = pl.program_id(0)
    j = pl.program_id(1)
    tm, tn = o_ref.shape

    @pl.when(i % tiles_per_batch == 0)
    def _():
        gs_ref[0:V7X_SUBLANES, :] = jnp.zeros((V7X_SUBLANES, tn), F32)

    @pl.when(i % tiles_per_batch != 0)
    def _():
        gs_ref[0:V7X_SUBLANES, :] = halo_ref[j]

    for c in range(tm // row_chunk):
        rows = slice(c * row_chunk, (c + 1) * row_chunk)
        base = V7X_SUBLANES + c * row_chunk
        h = h_ref[rows, :]
        gs_ref[base:base + row_chunk, :] = jnp.dot(h, wg_ref[...], preferred_element_type=F32)
        vs_ref[rows, :] = jnp.dot(h, wv_ref[...], preferred_element_type=F32)
        for s in range(tn // V7X_LANES):
            cs = slice(s * V7X_LANES, (s + 1) * V7X_LANES)
            gate = cb_ref[:, cs] + cw_ref[CONV_FFN - 1:CONV_FFN, cs] * gs_ref[base:base + row_chunk, cs]
            for t in range(CONV_FFN - 1):
                r0 = base - (CONV_FFN - 1) + t
                gate = gate + cw_ref[t:t + 1, cs] * gs_ref[r0:r0 + row_chunk, cs]
            o_ref[rows, cs] = (jax.nn.gelu(gate) * vs_ref[rows, cs]).astype(o_ref.dtype)
    halo_ref[j] = gs_ref[tm:tm + V7X_SUBLANES, :]


def _ffn_up(h2, w_up, conv_w, conv_b, *, seq, tm, tn, row_chunk):
    M, D = h2.shape
    d_ff = w_up.shape[1] // 2
    nj = d_ff // tn
    vmem = (2 * (tm * D * 2 + 2 * D * tn * 2 + tm * tn * 2) + (tm + 8) * tn * 4 + nj * 8 * tn * 4
            + 6 * row_chunk * tn * 4)
    return pl.pallas_call(
        functools.partial(_ffn_up_body, tiles_per_batch=seq // tm, row_chunk=row_chunk),
        grid=(M // tm, nj),
        in_specs=[pl.BlockSpec((tm, D), lambda i, j: (i, 0)),
                  pl.BlockSpec((D, tn), lambda i, j: (0, j)),
                  pl.BlockSpec((D, tn), lambda i, j: (0, j + nj)),
                  pl.BlockSpec((CONV_FFN, tn), lambda i, j: (0, j)),
                  pl.BlockSpec((1, tn), lambda i, j: (0, j))],
        out_specs=pl.BlockSpec((tm, tn), lambda i, j: (i, j)),
        out_shape=jax.ShapeDtypeStruct((M, d_ff), BF16),
        scratch_shapes=[pltpu.VMEM((tm + V7X_SUBLANES, tn), F32),
                        pltpu.VMEM((tm, tn), F32),
                        pltpu.VMEM((nj, V7X_SUBLANES, tn), F32)],
        compiler_params=_params(2, vmem),
        name="ffn_up",
    )(h2, w_up, w_up, conv_w, conv_b)


def _ffn_down_body(a_ref, w_ref, x_ref, g_ref, o_ref, *, final_norm, row_chunk):
    for c in range(a_ref.shape[0] // row_chunk):
        rows = slice(c * row_chunk, (c + 1) * row_chunk)
        x2 = x_ref[rows, :] + jnp.dot(a_ref[rows, :], w_ref[...], preferred_element_type=F32)
        o_ref[rows, :] = _rms(x2, g_ref[...]) if final_norm else x2


def _ffn_down(a, w_down, x1, g, *, tm, row_chunk, final_norm):
    M, d_ff = a.shape
    D = w_down.shape[1]
    vmem = 2 * (tm * d_ff * 2 + 2 * tm * D * 4) + d_ff * D * 2 + 2 * tm * D * 4
    return pl.pallas_call(
        functools.partial(_ffn_down_body, final_norm=final_norm, row_chunk=row_chunk),
        grid=(M // tm,),
        in_specs=[pl.BlockSpec((tm, d_ff), lambda i: (i, 0)),
                  pl.BlockSpec((d_ff, D), lambda i: (0, 0), pipeline_mode=pl.Buffered(1)),
                  pl.BlockSpec((tm, D), lambda i: (i, 0)),
                  pl.BlockSpec((1, D), lambda i: (0, 0))],
        out_specs=pl.BlockSpec((tm, D), lambda i: (i, 0)),
        out_shape=jax.ShapeDtypeStruct((M, D), F32),
        compiler_params=_params(1, vmem),
        name="ffn_down",
    )(a, w_down, x1, g)


def _layer(x2d, p, *, batch, seq):
    d_rnn = p["conv_w"].shape[1]
    gw = HEADS_PER_GROUP * HEAD_DIM
    att_w = len(ATT_GROUPS) * gw
    off_q = 2 * d_rnn
    off_g = off_q + 3 * att_w
    w_in = p["w_in"].astype(BF16)

    h = _norm(x2d, p["norm1_g"][None, :], tm=512)

    band = _lru_band(d_rnn, d_rnn // LRU_BLOCKS)
    y_rec = _lru(h, w_in[:, :off_q], p["conv_w"], p["conv_b"][None, :], _band_tiles(p["lru_wr"], band),
                 _band_tiles(p["lru_wi"], band), p["lru_br"][None, :], p["lru_bi"][None, :], p["lru_lambda"][None, :],
                 seq=seq, ts=256, tiles=2, band=band)

    slopes = _alibi_slopes(len(ATT_GROUPS) * HEADS_PER_GROUP).reshape(len(ATT_GROUPS), HEADS_PER_GROUP)
    att_parts, qkv_prev, att_prev = [], None, None
    for g, (window, dilation) in enumerate(ATT_GROUPS):
        w_g = jnp.concatenate([w_in[:, off_q + sec * att_w + g * gw: off_q + sec * att_w + (g + 1) * gw]
                               for sec in range(3)], axis=1)
        if qkv_prev is None:
            qkv = _matmul(h, w_g, tm=2048, tn=1024, name=f"inproj_qkv_d{dilation}")
        else:
            qkv, parts = _project_hosting_attention(
                h, w_g, qkv_prev, tm=2048, tn=768, dilation=dilation, nblk=4,
                name=f"inproj_qkv_d{dilation}_attn_d{att_prev['dilation']}", batch=batch, seq=seq,
                att_dilation=att_prev["dilation"], window=att_prev["window"], slopes=att_prev["slopes"])
            att_parts += parts
        qkv_prev, att_prev = qkv, dict(window=window, dilation=dilation, slopes=slopes[g])
    gates, parts = _project_hosting_attention(
        h, w_in[:, off_g:], qkv_prev, tm=1024, tn=1024, act="sigmoid", nblk=2,
        name=f"inproj_gates_attn_d{att_prev['dilation']}", batch=batch, seq=seq,
        att_dilation=att_prev["dilation"], window=att_prev["window"], slopes=att_prev["slopes"])
    att_parts += parts

    mixed = _mix(y_rec, att_parts, gates, p["w_rnn_out"].astype(BF16), p["w_att_out"].astype(BF16),
                 tm=512, row_chunk=512)
    x1, h2 = _outproj(mixed, p["w_out"].astype(BF16), x2d, p["norm2_g"][None, :], tm=512, row_chunk=512)
    hmid = _ffn_up(h2, p["w_up"].astype(BF16), p["ffn_conv_w"], p["ffn_conv_b"][None, :], seq=seq,
                   tm=2048, tn=512, row_chunk=1024)
    return x1, hmid


def kernel(x, norm1_g, w_in, conv_w, conv_b, lru_wr, lru_br, lru_wi, lru_bi, lru_lambda, w_rnn_out, w_att_out, w_out,
           norm2_g, w_up, ffn_conv_w, ffn_conv_b, w_down, final_g):
    B, S, D = x.shape
    depth = w_in.shape[0]
    x2d = x.reshape(B * S, D)
    for l in range(depth):
        p = dict(norm1_g=norm1_g[l], w_in=w_in[l], conv_w=conv_w[l], conv_b=conv_b[l], lru_wr=lru_wr[l],
                 lru_br=lru_br[l], lru_wi=lru_wi[l], lru_bi=lru_bi[l], lru_lambda=lru_lambda[l],
                 w_rnn_out=w_rnn_out[l], w_att_out=w_att_out[l], w_out=w_out[l], norm2_g=norm2_g[l],
                 w_up=w_up[l], ffn_conv_w=ffn_conv_w[l], ffn_conv_b=ffn_conv_b[l])
        x1, hmid = _layer(x2d, p, batch=B, seq=S)
        x2d = _ffn_down(hmid, w_down[l].astype(BF16), x1, final_g[None, :], tm=512, row_chunk=512,
                        final_norm=l == depth - 1)
    return x2d.reshape(B, S, D)
```

```python
import functools
import math

import numpy as np
import jax
import jax.numpy as jnp
from jax import lax
from jax.experimental import pallas as pl
from jax.experimental.pallas import tpu as pltpu

LRU_BLOCKS = 16
LRU_C = 8.0
CONV_REC = 4
ATT_GROUPS = ((128, 1), (512, 4), (2048, 16))
HEADS_PER_GROUP = 8
HEAD_DIM = 128
Q_BLOCK = 128
CONV_FFN = 3
EPS = 1e-6

V7X_LANES = 128
V7X_SUBLANES = 8
V7X_MXU_DIM = 256
V7X_VMEM_BYTES = 64 * 1024 * 1024

F32 = jnp.float32
BF16 = jnp.bfloat16


def _params(n_axes, vmem_bytes):
    limit = min(int(vmem_bytes * 1.25) + (4 << 20), V7X_VMEM_BYTES - (4 << 20))
    return pltpu.CompilerParams(dimension_semantics=("arbitrary",) * n_axes, vmem_limit_bytes=limit)


def _rms(xf, g):
    return xf * lax.rsqrt(jnp.mean(xf * xf, axis=-1, keepdims=True) + EPS) * g


def _mm_body(a_ref, w_ref, o_ref):
    o_ref[...] = jnp.dot(a_ref[...], w_ref[...], preferred_element_type=F32).astype(o_ref.dtype)


BF16_TILE_ROWS = 16


def _subseq_perm(dilation):
    lb = BF16_TILE_ROWS * dilation
    assert V7X_MXU_DIM % lb == 0
    p = np.arange(V7X_MXU_DIM)
    q = p % lb
    perm = np.zeros((V7X_MXU_DIM, V7X_MXU_DIM), np.float32)
    perm[p, p // lb * lb + (q % BF16_TILE_ROWS) * dilation + q // BF16_TILE_ROWS] = 1.0
    return perm


def _reorder_rows(a_ref, p_ref, asub_ref, dilation):
    tm = a_ref.shape[0]
    lb = BF16_TILE_ROWS * dilation
    chunk = dilation * Q_BLOCK
    for blk in range(tm // V7X_MXU_DIM):
        t = jnp.dot(p_ref[...], a_ref[blk * V7X_MXU_DIM:(blk + 1) * V7X_MXU_DIM, :],
                    preferred_element_type=F32).astype(BF16)
        for l in range(V7X_MXU_DIM // lb):
            g = blk * (V7X_MXU_DIM // lb) + l
            base = (g // V7X_SUBLANES) * chunk + (g % V7X_SUBLANES) * BF16_TILE_ROWS
            for r in range(dilation):
                src = l * lb + r * BF16_TILE_ROWS
                dst = base + r * Q_BLOCK
                asub_ref[dst:dst + BF16_TILE_ROWS, :] = t[src:src + BF16_TILE_ROWS, :]


def _matmul(a, w, cols, *, tm, tn, name):
    M, K = a.shape
    col0, N = cols
    assert M % tm == 0 and N % tn == 0 and col0 % tn == 0
    return pl.pallas_call(
        _mm_body,
        grid=(M // tm, N // tn),
        in_specs=[pl.BlockSpec((tm, K), lambda i, j: (i, 0)),
                  pl.BlockSpec((K, tn), lambda i, j: (0, j + col0 // tn))],
        out_specs=pl.BlockSpec((tm, tn), lambda i, j: (i, j)),
        out_shape=jax.ShapeDtypeStruct((M, N), BF16),
        compiler_params=_params(2, 2 * (tm * K * 2 + K * tn * 2 + tm * tn * 2) + tm * tn * 4),
        name=name,
    )(a, w)


def _lru_band(d_rnn, block):
    band = []
    for c0 in range(0, d_rnn, V7X_MXU_DIM):
        cn = min(V7X_MXU_DIM, d_rnn - c0)
        b0, b1 = c0 // block, (c0 + cn - 1) // block
        r0 = (block * b0) // V7X_LANES * V7X_LANES
        r1 = min(-(-(block * (b1 + 1)) // V7X_LANES) * V7X_LANES, d_rnn)
        band.append((c0, cn, r0, r1 - r0))
    return tuple(band)


def _band_tiles(w, band):
    dense = jax.scipy.linalg.block_diag(*[w[n] for n in range(w.shape[0])])
    kmax = max(k for _, _, _, k in band)
    tiles = []
    for c0, cn, r0, k in band:
        t = dense[r0:r0 + k, c0:c0 + cn]
        tiles.append(jnp.pad(t, ((0, kmax - k), (0, V7X_MXU_DIM - cn))))
    return jnp.stack(tiles).astype(BF16)


def _segment_order(ts):
    p = np.arange(ts)
    perm = np.zeros((ts, ts), np.float32)
    perm[p, (p % V7X_SUBLANES) * (ts // V7X_SUBLANES) + p // V7X_SUBLANES] = 1.0
    return perm


def _shift_sublanes(v, s, fill):
    row = lax.broadcasted_iota(jnp.int32, v.shape, 0)
    return jnp.where(row < s, fill, pltpu.roll(v, s, axis=0))


_LRU_HALO = (CONV_REC - 1) * V7X_SUBLANES


def _lru_body(x_ref, g1_ref, pin_ref, pout_ref, w_ref, cw_ref, cb_ref, wr_ref, wi_ref, br_ref, bi_ref, lam_ref,
              y_ref, h_ref, hil_ref, xs_ref, xf_ref, xb_ref, gs_ref, tail_ref, hc_ref, *, ts, steps_per_batch, band):
    @pl.when(pl.program_id(0) % steps_per_batch == 0)
    def _():
        tail_ref[...] = jnp.zeros(tail_ref.shape, F32)
        hc_ref[...] = jnp.zeros(hc_ref.shape, F32)

    for t in range(x_ref.shape[0] // ts):
        rows = slice(t * ts, (t + 1) * ts)
        h_ref[rows, :] = _rms(x_ref[rows, :], g1_ref[...]).astype(h_ref.dtype)
        _lru_tile(h_ref.at[rows], pin_ref, pout_ref, w_ref, cw_ref, cb_ref, wr_ref, wi_ref, br_ref, bi_ref, lam_ref,
                  y_ref.at[rows], hil_ref.at[t], xs_ref.at[t], xf_ref.at[t], xb_ref.at[t], gs_ref.at[t],
                  tail_ref, hc_ref, ts=ts, band=band)


def _lru_tile(h_ref, pin_ref, pout_ref, w_ref, cw_ref, cb_ref, wr_ref, wi_ref, br_ref, bi_ref, lam_ref, y_ref,
              hil_ref, xs_ref, xf_ref, xb_ref, gs_ref, tail_ref, hc_ref, *, ts, band):
    d_rnn = y_ref.shape[1]
    nseg = V7X_SUBLANES
    seg_len = ts // nseg

    hil_ref[...] = jnp.dot(pin_ref[...], h_ref[...], preferred_element_type=F32).astype(BF16)

    row8 = lax.broadcasted_iota(jnp.int32, (V7X_SUBLANES, V7X_MXU_DIM), 0)

    def project(c0, cn):
        cs = slice(c0, c0 + cn)
        xs_ref[_LRU_HALO:_LRU_HALO + ts, cs] = jnp.dot(hil_ref[...], w_ref[:, cs], preferred_element_type=F32)
        gs_ref[:, cs] = jnp.dot(hil_ref[...], w_ref[:, d_rnn + c0:d_rnn + c0 + cn], preferred_element_type=F32)
        for m in range(CONV_REC - 1):
            r0 = ts + m * V7X_SUBLANES
            cur8 = pltpu.roll(xs_ref[r0:r0 + V7X_SUBLANES, cs], 1, axis=0)
            prev8 = pltpu.roll(tail_ref[m * V7X_SUBLANES:(m + 1) * V7X_SUBLANES, cs], 1, axis=0)
            xs_ref[m * V7X_SUBLANES:(m + 1) * V7X_SUBLANES, cs] = jnp.where(row8[:, :cn] == 0, prev8, cur8)
        tail_ref[:, cs] = xs_ref[ts:ts + _LRU_HALO, cs]
        acc = cb_ref[:, cs] + cw_ref[CONV_REC - 1:CONV_REC, cs] * xs_ref[_LRU_HALO:_LRU_HALO + ts, cs]
        for i in range(CONV_REC - 1):
            r0 = i * V7X_SUBLANES
            acc = acc + cw_ref[i:i + 1, cs] * xs_ref[r0:r0 + ts, cs]
        xf_ref[:, cs] = acc
        xb_ref[:, cs] = acc.astype(BF16)

    def emit(cols, y_il):
        y_ref[:, cols] = jnp.dot(pout_ref[...], y_il, preferred_element_type=F32).astype(y_ref.dtype)

    projected = 0
    pending = None
    for j, (c0, cn, r0, k) in enumerate(band):
        ahead = band[min(j + 1, len(band) - 1)]
        while projected < len(band) and band[projected][0] < ahead[2] + ahead[3]:
            project(*band[projected][:2])
            projected += 1
        cs = slice(c0, c0 + cn)
        win = xb_ref[:, r0:r0 + k]
        zr = jnp.dot(win, wr_ref[j, :k, :], preferred_element_type=F32)[:, :cn] + br_ref[:, cs]
        zi = jnp.dot(win, wi_ref[j, :k, :], preferred_element_type=F32)[:, :cn] + bi_ref[:, cs]
        if pending is not None:
            emit(*pending)
        r = jax.nn.sigmoid(zr)
        gi = jax.nn.sigmoid(zi)
        log_a = -LRU_C * r * jax.nn.softplus(-lam_ref[:, cs])
        a = jnp.exp(log_a)
        b = jnp.sqrt(1.0 - jnp.exp(2.0 * log_a)) * (gi * xf_ref[:, cs])
        v = lambda x, i: x[i * V7X_SUBLANES:(i + 1) * V7X_SUBLANES]
        h_loc, a_cum = [v(b, 0)], [v(a, 0)]
        for i in range(1, seg_len):
            h_loc.append(v(a, i) * h_loc[-1] + v(b, i))
            a_cum.append(v(a, i) * a_cum[-1])
        a_inc, h_inc = a_cum[-1], h_loc[-1]
        s = 1
        while s < nseg:
            h_inc = a_inc * _shift_sublanes(h_inc, s, 0.0) + h_inc
            a_inc = a_inc * _shift_sublanes(a_inc, s, 1.0)
            s *= 2
        c_in = hc_ref[0:1, cs]
        c_end = a_inc * c_in + h_inc
        c_seg = _shift_sublanes(c_end, 1, c_in)
        hc_ref[0:1, cs] = c_end[nseg - 1:nseg, :]
        h = jnp.concatenate([a_cum[i] * c_seg + h_loc[i] for i in range(seg_len)], axis=0)
        pending = (cs, (h * jax.nn.gelu(gs_ref[:, cs])).astype(BF16))
    emit(*pending)


def _lru(x, g1, w_rec, conv_w, conv_b, wr_t, wi_t, br, bi, lam, *, seq, ts, tiles, band):
    M, D = x.shape
    d_rnn = conv_w.shape[1]
    nt, kmax, _ = wr_t.shape
    perm = _segment_order(ts)
    rows = tiles * ts
    row = lambda i: (0, 0)
    resident = lambda shape: pl.BlockSpec(shape, lambda i: (0,) * len(shape), pipeline_mode=pl.Buffered(1))
    vmem = (D * 2 * d_rnn * 2 + 2 * nt * kmax * V7X_MXU_DIM * 2 + 2 * rows * (D * 6 + d_rnn * 2)
            + tiles * (ts * D * 2 + (ts + _LRU_HALO) * d_rnn * 4 + ts * d_rnn * 10) + 24 * ts * V7X_MXU_DIM * 4)
    return pl.pallas_call(
        functools.partial(_lru_body, ts=ts, steps_per_batch=seq // rows, band=band),
        grid=(M // rows,),
        in_specs=[
            pl.BlockSpec((rows, D), lambda i: (i, 0)),
            pl.BlockSpec((1, D), row),
            pl.BlockSpec((ts, ts), row),
            pl.BlockSpec((ts, ts), row),
            resident((D, 2 * d_rnn)),
            pl.BlockSpec((CONV_REC, d_rnn), row),
            pl.BlockSpec((1, d_rnn), row),
            resident((nt, kmax, V7X_MXU_DIM)),
            resident((nt, kmax, V7X_MXU_DIM)),
            pl.BlockSpec((1, d_rnn), row),
            pl.BlockSpec((1, d_rnn), row),
            pl.BlockSpec((1, d_rnn), row),
        ],
        out_specs=[pl.BlockSpec((rows, d_rnn), lambda i: (i, 0)), pl.BlockSpec((rows, D), lambda i: (i, 0))],
        out_shape=[jax.ShapeDtypeStruct((M, d_rnn), BF16), jax.ShapeDtypeStruct((M, D), BF16)],
        scratch_shapes=[
            pltpu.VMEM((tiles, ts, D), BF16),
            pltpu.VMEM((tiles, ts + _LRU_HALO, d_rnn), F32),
            pltpu.VMEM((tiles, ts, d_rnn), F32),
            pltpu.VMEM((tiles, ts, d_rnn), BF16),
            pltpu.VMEM((tiles, ts, d_rnn), F32),
            pltpu.VMEM((_LRU_HALO, d_rnn), F32),
            pltpu.VMEM((V7X_SUBLANES, d_rnn), F32),
        ],
        compiler_params=_params(1, vmem),
        name="rglru",
    )(x, g1, jnp.asarray(perm, BF16), jnp.asarray(perm.T, BF16), w_rec, conv_w, conv_b, wr_t, wi_t, br, bi, lam)


def _alibi_slopes(n):
    def pow2_slopes(m):
        start = 2.0 ** (-8.0 / m)
        return [start ** (i + 1) for i in range(m)]
    c = 2 ** int(math.floor(math.log2(n)))
    s = pow2_slopes(c) + pow2_slopes(2 * c)[0::2][: n - c]
    return np.sort(np.asarray(s, np.float32))[::-1].copy()


def _attn_bias(slopes, dilation, span):
    qi = np.arange(Q_BLOCK)[:, None]
    ki = np.arange(2 * Q_BLOCK)[None, :]
    steps = qi + Q_BLOCK - ki
    band = (steps >= 0) & (steps <= span)
    bias = -np.asarray(slopes, np.float32)[:, None, None] * (steps * dilation).astype(np.float32)[None]
    with_prev = np.where(band[None], bias, -np.inf)
    no_prev = np.where((band & (ki >= Q_BLOCK))[None], bias, -np.inf)
    return np.stack([no_prev, with_prev]).astype(np.float32)


def _attn_block(i, n, rstep, q_ref, kc_ref, kp_ref, vc_ref, vp_ref, bias_ref, o_ref, l_ref, *, dilation, nblk,
                filler=None):
    scale = HEAD_DIM ** -0.5
    lane = lax.broadcasted_iota(jnp.int32, (Q_BLOCK, HEAD_DIM), 1)
    rows = slice(i * Q_BLOCK, (i + 1) * Q_BLOCK)
    if dilation == 1:
        has_prev = n > 0 if i == 0 else True
        out_rows = rows
    else:
        has_prev = n > 0
        out_rows = pl.ds(rstep * nblk + i, Q_BLOCK, stride=dilation)
    sel = 1 if has_prev is True else jnp.where(has_prev, 1, 0)
    def keys_values(ref_prev, ref_cur, hs):
        if dilation == 1 and i > 0:
            return ref_cur[(i - 1) * Q_BLOCK:(i + 1) * Q_BLOCK, hs]
        prev_rows = slice(0, Q_BLOCK) if dilation == 1 else rows
        return jnp.concatenate([ref_prev[prev_rows, hs], ref_cur[rows, hs]], axis=0)

    def scores(h):
        hs = slice(h * HEAD_DIM, (h + 1) * HEAD_DIM)
        return lax.dot_general(q_ref[rows, hs], keys_values(kp_ref, kc_ref, hs), (((1,), (1,)), ((), ())),
                               preferred_element_type=F32)

    lse = jnp.zeros((Q_BLOCK, HEAD_DIM), F32)
    s_next = scores(0)
    for h in range(HEADS_PER_GROUP):
        s = s_next
        if h + 1 < HEADS_PER_GROUP:
            s_next = scores(h + 1)
        if filler is not None:
            filler(h)
        s = s * scale + bias_ref[sel, h]
        m = jnp.max(s, axis=-1, keepdims=True)
        p = jnp.exp(s - m)
        den = jnp.sum(p, axis=-1, keepdims=True)
        vcat = keys_values(vp_ref, vc_ref, slice(h * HEAD_DIM, (h + 1) * HEAD_DIM))
        o = jnp.dot(p.astype(BF16), vcat, preferred_element_type=F32) / den
        o_ref[h, out_rows, :] = o
        lse = jnp.where(lane == h, m + jnp.log(den), lse)
    l_ref[out_rows, :] = lse


def _attn_plan(qkv, *, batch, seq, window, dilation, slopes, nblk):
    M, width = qkv.shape
    gw = HEADS_PER_GROUP * HEAD_DIM
    assert width == 3 * gw and seq % (dilation * Q_BLOCK) == 0
    nb = seq // (dilation * Q_BLOCK)
    blocks_per_batch = nb * dilation
    bias = jnp.asarray(_attn_bias(slopes, dilation, window // dilation))
    blk = (nblk * Q_BLOCK, gw)
    if dilation == 1:
        assert nb % nblk == 0
        grid = (batch, nb // nblk, 1)
        first = lambda b, n, r: b * blocks_per_batch + n * nblk
        cur = lambda sec: (lambda b, n, r: (first(b, n, r) // nblk, sec))
        prev = lambda sec: (lambda b, n, r: (jnp.maximum(first(b, n, r) - 1, 0), sec))
        prev_blk = (Q_BLOCK, gw)
        out_rows = nblk * Q_BLOCK
        out_idx = lambda b, n, r: b * (nb // nblk) + n
    else:
        assert dilation % nblk == 0
        grid = (batch, nb, dilation // nblk)
        first = lambda b, n, r: b * blocks_per_batch + n * dilation + r * nblk
        cur = lambda sec: (lambda b, n, r: (first(b, n, r) // nblk, sec))
        prev = lambda sec: (lambda b, n, r: (first(b, jnp.maximum(n - 1, 0), r) // nblk, sec))
        prev_blk = blk
        out_rows = dilation * Q_BLOCK
        out_idx = lambda b, n, r: b * nb + n
    return dict(
        grid=grid,
        operands=(qkv, qkv, qkv, qkv, qkv, bias),
        in_blocks=[blk, blk, prev_blk, blk, prev_blk, bias.shape],
        in_maps=[cur(0), cur(1), prev(1), cur(2), prev(2), lambda b, n, r: (0, 0, 0, 0)],
        out_blocks=[(HEADS_PER_GROUP, out_rows, HEAD_DIM), (out_rows, HEAD_DIM)],
        out_maps=[lambda b, n, r: (0, out_idx(b, n, r), 0), lambda b, n, r: (out_idx(b, n, r), 0)],
        out_shape=[jax.ShapeDtypeStruct((HEADS_PER_GROUP, M, HEAD_DIM), F32), jax.ShapeDtypeStruct((M, HEAD_DIM), F32)],
        vmem=(2 * (3 * nblk * Q_BLOCK + 2 * prev_blk[0]) * gw * 2 + 2 * out_rows * (gw + HEAD_DIM) * 4
              + 2 * bias.size * 4 + (4 << 20)),
    )


def _sigmoid(x):
    return 0.5 * (jnp.tanh(0.5 * x) + 1.0)


def _host_body(*refs, dilation, act, att_dilation, nblk, att_grid):
    if dilation > 1:
        a_ref, p_ref, w_ref, *att_in, o_ref, ao_ref, al_ref, asub_ref = refs
    else:
        a_ref, w_ref, *att_in, o_ref, ao_ref, al_ref = refs
        asub_ref = a_ref
    i, j = pl.program_id(0), pl.program_id(1)

    if dilation > 1:
        @pl.when(j == 0)
        def _():
            _reorder_rows(a_ref, p_ref, asub_ref, dilation)

    _, gn, gr = att_grid
    step = i * pl.num_programs(1) + j
    n, rstep = (step // gr) % gn, step % gr
    rc = a_ref.shape[0] // nblk
    ncol = o_ref.shape[1] // V7X_MXU_DIM
    every = -(-HEADS_PER_GROUP // ncol)
    for b in range(nblk):
        rows = slice(b * rc, (b + 1) * rc)

        def project_piece(h, rows=rows):
            if h % every == 0 and h // every < ncol:
                cols = slice(h // every * V7X_MXU_DIM, (h // every + 1) * V7X_MXU_DIM)
                acc = jnp.dot(asub_ref[rows, :], w_ref[:, cols], preferred_element_type=F32)
                o_ref[rows, cols] = (_sigmoid(acc) if act == "sigmoid" else acc).astype(o_ref.dtype)

        _attn_block(b, n, rstep, *att_in, ao_ref, al_ref, dilation=att_dilation, nblk=nblk, filler=project_piece)


def _project_hosting_attention(a, w, cols, qkv_att, *, tm, tn, name, att_dilation, nblk, dilation=1, act=None,
                               **att_kw):
    M, K = a.shape
    col0, N = cols
    nj = N // tn
    assert M % tm == 0 and N % tn == 0 and col0 % tn == 0 and tm % (dilation * Q_BLOCK) == 0
    assert tn // V7X_MXU_DIM <= HEADS_PER_GROUP
    plan = _attn_plan(qkv_att, dilation=att_dilation, nblk=nblk, **att_kw)
    gb, gn, gr = plan["grid"]
    assert (M // tm) * nj == gb * gn * gr

    def hosted(index_map):
        def host_map(i, j):
            step = i * nj + j
            return index_map(step // (gn * gr), (step // gr) % gn, step % gr)
        return host_map

    in_specs = [pl.BlockSpec((tm, K), lambda i, j: (i, 0))]
    operands = [a]
    scratch = []
    vmem = 2 * (tm * K * 2 + K * tn * 2 + tm * tn * 2) + (tm // nblk) * tn * 4 + plan["vmem"]
    if dilation > 1:
        perm = jnp.asarray(_subseq_perm(dilation), BF16)
        in_specs.append(pl.BlockSpec(perm.shape, lambda i, j: (0, 0)))
        operands.append(perm)
        scratch = [pltpu.VMEM((tm, K), BF16)]
        vmem += tm * K * 2
    in_specs.append(pl.BlockSpec((K, tn), lambda i, j: (0, j + col0 // tn)))
    outs = pl.pallas_call(
        functools.partial(_host_body, dilation=dilation, act=act, att_dilation=att_dilation, nblk=nblk,
                          att_grid=plan["grid"]),
        grid=(M // tm, nj),
        in_specs=in_specs + [pl.BlockSpec(b, hosted(m)) for b, m in zip(plan["in_blocks"], plan["in_maps"])],
        out_specs=[pl.BlockSpec((tm, tn), lambda i, j: (i, j))]
                  + [pl.BlockSpec(b, hosted(m)) for b, m in zip(plan["out_blocks"], plan["out_maps"])],
        out_shape=[jax.ShapeDtypeStruct((M, N), BF16)] + plan["out_shape"],
        scratch_shapes=scratch,
        compiler_params=_params(2, vmem),
        name=name,
    )(*operands, w, *plan["operands"])
    return outs[0], outs[1:]


def _mix_body(y_ref, o1_ref, l1_ref, o2_ref, l2_ref, o3_ref, l3_ref, gr_ref, ga_ref, wr_ref, wa_ref,
              out_ref, *, row_chunk):
    for c in range(y_ref.shape[0] // row_chunk):
        rows = slice(c * row_chunk, (c + 1) * row_chunk)
        l1, l2, l3 = l1_ref[rows, :], l2_ref[rows, :], l3_ref[rows, :]
        m = jnp.maximum(jnp.maximum(l1, l2), l3)
        e1, e2, e3 = jnp.exp(l1 - m), jnp.exp(l2 - m), jnp.exp(l3 - m)
        den = e1 + e2 + e3
        w1, w2, w3 = e1 / den, e2 / den, e3 / den
        heads = []
        for h in range(HEADS_PER_GROUP):
            bc = lambda w: jnp.broadcast_to(w[:, h:h + 1], (row_chunk, HEAD_DIM))
            att = bc(w1) * o1_ref[h, rows, :] + bc(w2) * o2_ref[h, rows, :] + bc(w3) * o3_ref[h, rows, :]
            heads.append(att.astype(BF16))
        att = jnp.concatenate(heads, axis=1)
        ya = jnp.dot(y_ref[rows, :], wr_ref[...], preferred_element_type=F32)
        yb = jnp.dot(att, wa_ref[...], preferred_element_type=F32)
        out_ref[rows, :] = (gr_ref[rows, :].astype(F32) * ya + ga_ref[rows, :].astype(F32) * yb).astype(out_ref.dtype)


def _mix(y_rec, att_parts, gates, w_rnn, w_att, *, tm, row_chunk):
    M, d_rnn = y_rec.shape
    aw, D = w_att.shape
    rowblk = lambda w: pl.BlockSpec((tm, w), lambda i: (i, 0))
    headblk = pl.BlockSpec((HEADS_PER_GROUP, tm, HEAD_DIM), lambda i: (0, i, 0))
    resident = lambda r: pl.BlockSpec((r, D), lambda i: (0, 0), pipeline_mode=pl.Buffered(1))
    vmem = (2 * (tm * d_rnn * 2 + 3 * tm * (aw + HEAD_DIM) * 4 + 2 * tm * D * 2 + tm * D * 2) + (d_rnn + aw) * D * 2
            + 4 * row_chunk * D * 4)
    return pl.pallas_call(
        functools.partial(_mix_body, row_chunk=row_chunk),
        grid=(M // tm,),
        in_specs=[rowblk(d_rnn)] + [headblk, rowblk(HEAD_DIM)] * 3 + [
            pl.BlockSpec((tm, D), lambda i: (i, 0)),
            pl.BlockSpec((tm, D), lambda i: (i, 1)),
            resident(d_rnn), resident(aw)],
        out_specs=rowblk(D),
        out_shape=jax.ShapeDtypeStruct((M, D), BF16),
        compiler_params=_params(1, vmem),
        name="mix",
    )(y_rec, *att_parts, gates, gates, w_rnn, w_att)


def _outproj_body(m_ref, w_ref, x_ref, g_ref, x1_ref, h2_ref, *, row_chunk):
    for c in range(m_ref.shape[0] // row_chunk):
        rows = slice(c * row_chunk, (c + 1) * row_chunk)
        x1 = x_ref[rows, :] + jnp.dot(m_ref[rows, :], w_ref[...], preferred_element_type=F32)
        x1_ref[rows, :] = x1
        h2_ref[rows, :] = _rms(x1, g_ref[...]).astype(h2_ref.dtype)


def _outproj(mixed, w_out, x, g2, *, tm, row_chunk):
    M, D = x.shape
    vmem = 2 * (tm * D * 2 + D * D * 2 + tm * D * 4 + tm * D * 4 + tm * D * 2) + 2 * tm * D * 4
    return pl.pallas_call(
        functools.partial(_outproj_body, row_chunk=row_chunk),
        grid=(M // tm,),
        in_specs=[pl.BlockSpec((tm, D), lambda i: (i, 0)), pl.BlockSpec((D, D), lambda i: (0, 0)),
                  pl.BlockSpec((tm, D), lambda i: (i, 0)), pl.BlockSpec((1, D), lambda i: (0, 0))],
        out_specs=[pl.BlockSpec((tm, D), lambda i: (i, 0))] * 2,
        out_shape=[jax.ShapeDtypeStruct((M, D), F32), jax.ShapeDtypeStruct((M, D), BF16)],
        compiler_params=_params(1, vmem),
        name="outproj",
    )(mixed, w_out, x, g2)


def _ffn_up_body(h_ref, wg_ref, wv_ref, cw_ref, cb_ref, o_ref, gs_ref, vs_ref, halo_ref, *, tiles_per_batch,
                 row_chunk):
    i = pl.program_id(0)
    j = pl.program_id(1)
    tm, tn = o_ref.shape

    @pl.when(i % tiles_per_batch == 0)
    def _():
        gs_ref[0:V7X_SUBLANES, :] = jnp.zeros((V7X_SUBLANES, tn), F32)

    @pl.when(i % tiles_per_batch != 0)
    def _():
        gs_ref[0:V7X_SUBLANES, :] = halo_ref[j]

    for c in range(tm // row_chunk):
        rows = slice(c * row_chunk, (c + 1) * row_chunk)
        base = V7X_SUBLANES + c * row_chunk
        h = h_ref[rows, :]
        gs_ref[base:base + row_chunk, :] = jnp.dot(h, wg_ref[...], preferred_element_type=F32)
        vs_ref[rows, :] = jnp.dot(h, wv_ref[...], preferred_element_type=F32)
        for s in range(tn // V7X_LANES):
            cs = slice(s * V7X_LANES, (s + 1) * V7X_LANES)
            gate = cb_ref[:, cs] + cw_ref[CONV_FFN - 1:CONV_FFN, cs] * gs_ref[base:base + row_chunk, cs]
            for t in range(CONV_FFN - 1):
                r0 = base - (CONV_FFN - 1) + t
                gate = gate + cw_ref[t:t + 1, cs] * gs_ref[r0:r0 + row_chunk, cs]
            o_ref[rows, cs] = (jax.nn.gelu(gate) * vs_ref[rows, cs]).astype(o_ref.dtype)
    halo_ref[j] = gs_ref[tm:tm + V7X_SUBLANES, :]


def _ffn_up(h2, w_up, conv_w, conv_b, *, seq, tm, tn, row_chunk):
    M, D = h2.shape
    d_ff = w_up.shape[1] // 2
    nj = d_ff // tn
    vmem = (2 * (tm * D * 2 + 2 * D * tn * 2 + tm * tn * 2) + (tm + 8) * tn * 4 + nj * 8 * tn * 4
            + 6 * row_chunk * tn * 4)
    return pl.pallas_call(
        functools.partial(_ffn_up_body, tiles_per_batch=seq // tm, row_chunk=row_chunk),
        grid=(M // tm, nj),
        in_specs=[pl.BlockSpec((tm, D), lambda i, j: (i, 0)),
                  pl.BlockSpec((D, tn), lambda i, j: (0, j)),
                  pl.BlockSpec((D, tn), lambda i, j: (0, j + nj)),
                  pl.BlockSpec((CONV_FFN, tn), lambda i, j: (0, j)),
                  pl.BlockSpec((1, tn), lambda i, j: (0, j))],
        out_specs=pl.BlockSpec((tm, tn), lambda i, j: (i, j)),
        out_shape=jax.ShapeDtypeStruct((M, d_ff), BF16),
        scratch_shapes=[pltpu.VMEM((tm + V7X_SUBLANES, tn), F32),
                        pltpu.VMEM((tm, tn), F32),
                        pltpu.VMEM((nj, V7X_SUBLANES, tn), F32)],
        compiler_params=_params(2, vmem),
        name="ffn_up",
    )(h2, w_up, w_up, conv_w, conv_b)


def _ffn_down_body(a_ref, w_ref, x_ref, g_ref, o_ref, *, final_norm, row_chunk):
    for c in range(a_ref.shape[0] // row_chunk):
        rows = slice(c * row_chunk, (c + 1) * row_chunk)
        x2 = x_ref[rows, :] + jnp.dot(a_ref[rows, :], w_ref[...], preferred_element_type=F32)
        o_ref[rows, :] = _rms(x2, g_ref[...]) if final_norm else x2


def _ffn_down(a, w_down, x1, g, *, tm, row_chunk, final_norm):
    M, d_ff = a.shape
    D = w_down.shape[1]
    vmem = 2 * (tm * d_ff * 2 + 2 * tm * D * 4) + d_ff * D * 2 + 2 * tm * D * 4
    return pl.pallas_call(
        functools.partial(_ffn_down_body, final_norm=final_norm, row_chunk=row_chunk),
        grid=(M // tm,),
        in_specs=[pl.BlockSpec((tm, d_ff), lambda i: (i, 0)),
                  pl.BlockSpec((d_ff, D), lambda i: (0, 0), pipeline_mode=pl.Buffered(1)),
                  pl.BlockSpec((tm, D), lambda i: (i, 0)),
                  pl.BlockSpec((1, D), lambda i: (0, 0))],
        out_specs=pl.BlockSpec((tm, D), lambda i: (i, 0)),
        out_shape=jax.ShapeDtypeStruct((M, D), F32),
        compiler_params=_params(1, vmem),
        name="ffn_down",
    )(a, w_down, x1, g)


def _layer(x2d, p, *, batch, seq):
    d_rnn = p["conv_w"].shape[1]
    gw = HEADS_PER_GROUP * HEAD_DIM
    att_w = len(ATT_GROUPS) * gw
    off_q = 2 * d_rnn
    off_g = off_q + 3 * att_w
    qkv_tn, gates_tn = 768, 1024
    w_f32 = p["w_in"]
    sections = [w_f32[:, :off_q]]
    for g in range(len(ATT_GROUPS)):
        sections += [w_f32[:, off_q + sec * att_w + g * gw: off_q + sec * att_w + (g + 1) * gw] for sec in range(3)]
    qkv_col0 = [off_q + g * 3 * gw for g in range(len(ATT_GROUPS))]
    gates_col0 = -(-off_g // gates_tn) * gates_tn
    sections += [jnp.zeros((w_f32.shape[0], gates_col0 - off_g), w_f32.dtype), w_f32[:, off_g:]]
    w_in = jnp.concatenate(sections, axis=1).astype(BF16)

    band = _lru_band(d_rnn, d_rnn // LRU_BLOCKS)
    y_rec, h = _lru(x2d, p["norm1_g"][None, :], w_in, p["conv_w"], p["conv_b"][None, :],
                    _band_tiles(p["lru_wr"], band), _band_tiles(p["lru_wi"], band), p["lru_br"][None, :],
                    p["lru_bi"][None, :], p["lru_lambda"][None, :], seq=seq, ts=256, tiles=2, band=band)

    slopes = _alibi_slopes(len(ATT_GROUPS) * HEADS_PER_GROUP).reshape(len(ATT_GROUPS), HEADS_PER_GROUP)
    att_parts, qkv_prev, att_prev = [], None, None
    for g, (window, dilation) in enumerate(ATT_GROUPS):
        cols = (qkv_col0[g], 3 * gw)
        if qkv_prev is None:
            qkv = _matmul(h, w_in, cols, tm=2048, tn=qkv_tn, name=f"inproj_qkv_d{dilation}")
        else:
            qkv, parts = _project_hosting_attention(
                h, w_in, cols, qkv_prev, tm=2048, tn=qkv_tn, dilation=dilation, nblk=4,
                name=f"inproj_qkv_d{dilation}_attn_d{att_prev['dilation']}", batch=batch, seq=seq,
                att_dilation=att_prev["dilation"], window=att_prev["window"], slopes=att_prev["slopes"])
            att_parts += parts
        qkv_prev, att_prev = qkv, dict(window=window, dilation=dilation, slopes=slopes[g])
    gates, parts = _project_hosting_attention(
        h, w_in, (gates_col0, w_f32.shape[1] - off_g), qkv_prev, tm=1024, tn=gates_tn, act="sigmoid", nblk=2,
        name=f"inproj_gates_attn_d{att_prev['dilation']}", batch=batch, seq=seq,
        att_dilation=att_prev["dilation"], window=att_prev["window"], slopes=att_prev["slopes"])
    att_parts += parts

    mixed = _mix(y_rec, att_parts, gates, p["w_rnn_out"].astype(BF16), p["w_att_out"].astype(BF16),
                 tm=512, row_chunk=512)
    x1, h2 = _outproj(mixed, p["w_out"].astype(BF16), x2d, p["norm2_g"][None, :], tm=512, row_chunk=512)
    hmid = _ffn_up(h2, p["w_up"].astype(BF16), p["ffn_conv_w"], p["ffn_conv_b"][None, :], seq=seq,
                   tm=2048, tn=512, row_chunk=1024)
    return x1, hmid


def kernel(x, norm1_g, w_in, conv_w, conv_b, lru_wr, lru_br, lru_wi, lru_bi, lru_lambda, w_rnn_out, w_att_out, w_out,
           norm2_g, w_up, ffn_conv_w, ffn_conv_b, w_down, final_g):
    B, S, D = x.shape
    depth = w_in.shape[0]
    x2d = x.reshape(B * S, D)
    for l in range(depth):
        p = dict(norm1_g=norm1_g[l], w_in=w_in[l], conv_w=conv_w[l], conv_b=conv_b[l], lru_wr=lru_wr[l],
                 lru_br=lru_br[l], lru_wi=lru_wi[l], lru_bi=lru_bi[l], lru_lambda=lru_lambda[l],
                 w_rnn_out=w_rnn_out[l], w_att_out=w_att_out[l], w_out=w_out[l], norm2_g=norm2_g[l],
                 w_up=w_up[l], ffn_conv_w=ffn_conv_w[l], ffn_conv_b=ffn_conv_b[l])
        x1, hmid = _layer(x2d, p, batch=B, seq=S)
        x2d = _ffn_down(hmid, w_down[l].astype(BF16), x1, final_g[None, :], tm=512, row_chunk=512,
                        final_norm=l == depth - 1)
    return x2d.reshape(B, S, D)
```

```python
import functools
import math

import numpy as np
import jax
import jax.numpy as jnp
from jax import lax
from jax.experimental import pallas as pl
from jax.experimental.pallas import tpu as pltpu

LRU_BLOCKS = 16
LRU_C = 8.0
CONV_REC = 4
ATT_GROUPS = ((128, 1), (512, 4), (2048, 16))
HEADS_PER_GROUP = 8
HEAD_DIM = 128
Q_BLOCK = 128
CONV_FFN = 3
EPS = 1e-6

V7X_LANES = 128
V7X_SUBLANES = 8
V7X_MXU_DIM = 256
V7X_VMEM_BYTES = 64 * 1024 * 1024

F32 = jnp.float32
BF16 = jnp.bfloat16


def _params(n_axes, vmem_bytes):
    limit = min(int(vmem_bytes * 1.25) + (4 << 20), V7X_VMEM_BYTES - (4 << 20))
    return pltpu.CompilerParams(dimension_semantics=("arbitrary",) * n_axes, vmem_limit_bytes=limit)


def _rms(xf, g):
    return xf * lax.rsqrt(jnp.mean(xf * xf, axis=-1, keepdims=True) + EPS) * g


def _mm_body(a_ref, w_ref, o_ref):
    o_ref[...] = jnp.dot(a_ref[...], w_ref[...], preferred_element_type=F32).astype(o_ref.dtype)


BF16_TILE_ROWS = 16


def _subseq_perm(dilation):
    lb = BF16_TILE_ROWS * dilation
    assert V7X_MXU_DIM % lb == 0
    p = np.arange(V7X_MXU_DIM)
    q = p % lb
    perm = np.zeros((V7X_MXU_DIM, V7X_MXU_DIM), np.float32)
    perm[p, p // lb * lb + (q % BF16_TILE_ROWS) * dilation + q // BF16_TILE_ROWS] = 1.0
    return perm


def _reorder_rows(a_ref, p_ref, asub_ref, dilation):
    tm = a_ref.shape[0]
    lb = BF16_TILE_ROWS * dilation
    chunk = dilation * Q_BLOCK
    for blk in range(tm // V7X_MXU_DIM):
        t = jnp.dot(p_ref[...], a_ref[blk * V7X_MXU_DIM:(blk + 1) * V7X_MXU_DIM, :],
                    preferred_element_type=F32).astype(BF16)
        for l in range(V7X_MXU_DIM // lb):
            g = blk * (V7X_MXU_DIM // lb) + l
            base = (g // V7X_SUBLANES) * chunk + (g % V7X_SUBLANES) * BF16_TILE_ROWS
            for r in range(dilation):
                src = l * lb + r * BF16_TILE_ROWS
                dst = base + r * Q_BLOCK
                asub_ref[dst:dst + BF16_TILE_ROWS, :] = t[src:src + BF16_TILE_ROWS, :]


def _matmul(a, w, cols, *, tm, tn, name):
    M, K = a.shape
    col0, N = cols
    assert M % tm == 0 and N % tn == 0 and col0 % tn == 0
    return pl.pallas_call(
        _mm_body,
        grid=(M // tm, N // tn),
        in_specs=[pl.BlockSpec((tm, K), lambda i, j: (i, 0)),
                  pl.BlockSpec((K, tn), lambda i, j: (0, j + col0 // tn))],
        out_specs=pl.BlockSpec((tm, tn), lambda i, j: (i, j)),
        out_shape=jax.ShapeDtypeStruct((M, N), BF16),
        compiler_params=_params(2, 2 * (tm * K * 2 + K * tn * 2 + tm * tn * 2) + tm * tn * 4),
        name=name,
    )(a, w)


def _lru_band(d_rnn, block):
    band = []
    for c0 in range(0, d_rnn, V7X_MXU_DIM):
        cn = min(V7X_MXU_DIM, d_rnn - c0)
        b0, b1 = c0 // block, (c0 + cn - 1) // block
        r0 = (block * b0) // V7X_LANES * V7X_LANES
        r1 = min(-(-(block * (b1 + 1)) // V7X_LANES) * V7X_LANES, d_rnn)
        band.append((c0, cn, r0, r1 - r0))
    return tuple(band)


def _band_tiles(w, band):
    dense = jax.scipy.linalg.block_diag(*[w[n] for n in range(w.shape[0])])
    kmax = max(k for _, _, _, k in band)
    tiles = []
    for c0, cn, r0, k in band:
        t = dense[r0:r0 + k, c0:c0 + cn]
        tiles.append(jnp.pad(t, ((0, kmax - k), (0, V7X_MXU_DIM - cn))))
    return jnp.stack(tiles).astype(BF16)


def _segment_order(ts):
    p = np.arange(ts)
    perm = np.zeros((ts, ts), np.float32)
    perm[p, (p % V7X_SUBLANES) * (ts // V7X_SUBLANES) + p // V7X_SUBLANES] = 1.0
    return perm


def _shift_sublanes(v, s, fill):
    row = lax.broadcasted_iota(jnp.int32, v.shape, 0)
    return jnp.where(row < s, fill, pltpu.roll(v, s, axis=0))


_LRU_HALO = (CONV_REC - 1) * V7X_SUBLANES


def _lru_body(x_ref, g1_ref, pin_ref, pout_ref, w_ref, cw_ref, cb_ref, wr_ref, wi_ref, br_ref, bi_ref, lam_ref,
              y_ref, h_ref, hil_ref, xs_ref, xf_ref, xb_ref, gs_ref, tail_ref, hc_ref, *, ts, steps_per_batch, band):
    @pl.when(pl.program_id(0) % steps_per_batch == 0)
    def _():
        tail_ref[...] = jnp.zeros(tail_ref.shape, F32)
        hc_ref[...] = jnp.zeros(hc_ref.shape, F32)

    for t in range(x_ref.shape[0] // ts):
        rows = slice(t * ts, (t + 1) * ts)
        h_ref[rows, :] = _rms(x_ref[rows, :], g1_ref[...]).astype(h_ref.dtype)
        _lru_tile(h_ref.at[rows], pin_ref, pout_ref, w_ref, cw_ref, cb_ref, wr_ref, wi_ref, br_ref, bi_ref, lam_ref,
                  y_ref.at[rows], hil_ref.at[t], xs_ref.at[t], xf_ref.at[t], xb_ref.at[t], gs_ref.at[t],
                  tail_ref, hc_ref, ts=ts, band=band)


def _lru_tile(h_ref, pin_ref, pout_ref, w_ref, cw_ref, cb_ref, wr_ref, wi_ref, br_ref, bi_ref, lam_ref, y_ref,
              hil_ref, xs_ref, xf_ref, xb_ref, gs_ref, tail_ref, hc_ref, *, ts, band):
    d_rnn = y_ref.shape[1]
    nseg = V7X_SUBLANES
    seg_len = ts // nseg

    hil_ref[...] = jnp.dot(pin_ref[...], h_ref[...], preferred_element_type=F32).astype(BF16)

    row8 = lax.broadcasted_iota(jnp.int32, (V7X_SUBLANES, V7X_MXU_DIM), 0)

    def project(c0, cn):
        cs = slice(c0, c0 + cn)
        xs_ref[_LRU_HALO:_LRU_HALO + ts, cs] = jnp.dot(hil_ref[...], w_ref[:, cs], preferred_element_type=F32)
        gs_ref[:, cs] = jnp.dot(hil_ref[...], w_ref[:, d_rnn + c0:d_rnn + c0 + cn], preferred_element_type=F32)
        for m in range(CONV_REC - 1):
            r0 = ts + m * V7X_SUBLANES
            cur8 = pltpu.roll(xs_ref[r0:r0 + V7X_SUBLANES, cs], 1, axis=0)
            prev8 = pltpu.roll(tail_ref[m * V7X_SUBLANES:(m + 1) * V7X_SUBLANES, cs], 1, axis=0)
            xs_ref[m * V7X_SUBLANES:(m + 1) * V7X_SUBLANES, cs] = jnp.where(row8[:, :cn] == 0, prev8, cur8)
        tail_ref[:, cs] = xs_ref[ts:ts + _LRU_HALO, cs]
        acc = cb_ref[:, cs] + cw_ref[CONV_REC - 1:CONV_REC, cs] * xs_ref[_LRU_HALO:_LRU_HALO + ts, cs]
        for i in range(CONV_REC - 1):
            r0 = i * V7X_SUBLANES
            acc = acc + cw_ref[i:i + 1, cs] * xs_ref[r0:r0 + ts, cs]
        xf_ref[:, cs] = acc
        xb_ref[:, cs] = acc.astype(BF16)

    def emit(cols, y_il):
        y_ref[:, cols] = jnp.dot(pout_ref[...], y_il, preferred_element_type=F32).astype(y_ref.dtype)

    projected = 0
    pending = None
    for j, (c0, cn, r0, k) in enumerate(band):
        ahead = band[min(j + 1, len(band) - 1)]
        while projected < len(band) and band[projected][0] < ahead[2] + ahead[3]:
            project(*band[projected][:2])
            projected += 1
        cs = slice(c0, c0 + cn)
        win = xb_ref[:, r0:r0 + k]
        zr = jnp.dot(win, wr_ref[j, :k, :], preferred_element_type=F32)[:, :cn] + br_ref[:, cs]
        zi = jnp.dot(win, wi_ref[j, :k, :], preferred_element_type=F32)[:, :cn] + bi_ref[:, cs]
        if pending is not None:
            emit(*pending)
        r = jax.nn.sigmoid(zr)
        gi = jax.nn.sigmoid(zi)
        log_a = -LRU_C * r * jax.nn.softplus(-lam_ref[:, cs])
        a = jnp.exp(log_a)
        b = jnp.sqrt(1.0 - jnp.exp(2.0 * log_a)) * (gi * xf_ref[:, cs])
        v = lambda x, i: x[i * V7X_SUBLANES:(i + 1) * V7X_SUBLANES]
        h_loc, a_cum = [v(b, 0)], [v(a, 0)]
        for i in range(1, seg_len):
            h_loc.append(v(a, i) * h_loc[-1] + v(b, i))
            a_cum.append(v(a, i) * a_cum[-1])
        a_inc, h_inc = a_cum[-1], h_loc[-1]
        s = 1
        while s < nseg:
            h_inc = a_inc * _shift_sublanes(h_inc, s, 0.0) + h_inc
            a_inc = a_inc * _shift_sublanes(a_inc, s, 1.0)
            s *= 2
        c_in = hc_ref[0:1, cs]
        c_end = a_inc * c_in + h_inc
        c_seg = _shift_sublanes(c_end, 1, c_in)
        hc_ref[0:1, cs] = c_end[nseg - 1:nseg, :]
        h = jnp.concatenate([a_cum[i] * c_seg + h_loc[i] for i in range(seg_len)], axis=0)
        pending = (cs, (h * jax.nn.gelu(gs_ref[:, cs])).astype(BF16))
    emit(*pending)


def _lru(x, g1, w_rec, conv_w, conv_b, wr_t, wi_t, br, bi, lam, *, seq, ts, tiles, band):
    M, D = x.shape
    d_rnn = conv_w.shape[1]
    nt, kmax, _ = wr_t.shape
    perm = _segment_order(ts)
    rows = tiles * ts
    row = lambda i: (0, 0)
    resident = lambda shape: pl.BlockSpec(shape, lambda i: (0,) * len(shape), pipeline_mode=pl.Buffered(1))
    vmem = (D * 2 * d_rnn * 2 + 2 * nt * kmax * V7X_MXU_DIM * 2 + 2 * rows * (D * 6 + d_rnn * 2)
            + tiles * (ts * D * 2 + (ts + _LRU_HALO) * d_rnn * 4 + ts * d_rnn * 10) + 24 * ts * V7X_MXU_DIM * 4)
    return pl.pallas_call(
        functools.partial(_lru_body, ts=ts, steps_per_batch=seq // rows, band=band),
        grid=(M // rows,),
        in_specs=[
            pl.BlockSpec((rows, D), lambda i: (i, 0)),
            pl.BlockSpec((1, D), row),
            pl.BlockSpec((ts, ts), row),
            pl.BlockSpec((ts, ts), row),
            resident((D, 2 * d_rnn)),
            pl.BlockSpec((CONV_REC, d_rnn), row),
            pl.BlockSpec((1, d_rnn), row),
            resident((nt, kmax, V7X_MXU_DIM)),
            resident((nt, kmax, V7X_MXU_DIM)),
            pl.BlockSpec((1, d_rnn), row),
            pl.BlockSpec((1, d_rnn), row),
            pl.BlockSpec((1, d_rnn), row),
        ],
        out_specs=[pl.BlockSpec((rows, d_rnn), lambda i: (i, 0)), pl.BlockSpec((rows, D), lambda i: (i, 0))],
        out_shape=[jax.ShapeDtypeStruct((M, d_rnn), BF16), jax.ShapeDtypeStruct((M, D), BF16)],
        scratch_shapes=[
            pltpu.VMEM((tiles, ts, D), BF16),
            pltpu.VMEM((tiles, ts + _LRU_HALO, d_rnn), F32),
            pltpu.VMEM((tiles, ts, d_rnn), F32),
            pltpu.VMEM((tiles, ts, d_rnn), BF16),
            pltpu.VMEM((tiles, ts, d_rnn), F32),
            pltpu.VMEM((_LRU_HALO, d_rnn), F32),
            pltpu.VMEM((V7X_SUBLANES, d_rnn), F32),
        ],
        compiler_params=_params(1, vmem),
        name="rglru",
    )(x, g1, jnp.asarray(perm, BF16), jnp.asarray(perm.T, BF16), w_rec, conv_w, conv_b, wr_t, wi_t, br, bi, lam)


def _alibi_slopes(n):
    def pow2_slopes(m):
        start = 2.0 ** (-8.0 / m)
        return [start ** (i + 1) for i in range(m)]
    c = 2 ** int(math.floor(math.log2(n)))
    s = pow2_slopes(c) + pow2_slopes(2 * c)[0::2][: n - c]
    return np.sort(np.asarray(s, np.float32))[::-1].copy()


def _attn_bias(slopes, dilation, span):
    qi = np.arange(Q_BLOCK)[:, None]
    ki = np.arange(2 * Q_BLOCK)[None, :]
    steps = qi + Q_BLOCK - ki
    band = (steps >= 0) & (steps <= span)
    bias = -np.asarray(slopes, np.float32)[:, None, None] * (steps * dilation).astype(np.float32)[None]
    with_prev = np.where(band[None], bias, -np.inf)
    no_prev = np.where((band & (ki >= Q_BLOCK))[None], bias, -np.inf)
    return np.stack([no_prev, with_prev]).astype(np.float32)


def _attn_block(i, n, rstep, q_ref, kc_ref, kp_ref, vc_ref, vp_ref, bias_ref, o_ref, l_ref, *, dilation, nblk,
                filler=None):
    scale = HEAD_DIM ** -0.5
    lane = lax.broadcasted_iota(jnp.int32, (Q_BLOCK, HEAD_DIM), 1)
    rows = slice(i * Q_BLOCK, (i + 1) * Q_BLOCK)
    if dilation == 1:
        has_prev = n > 0 if i == 0 else True
        out_rows = rows
    else:
        has_prev = n > 0
        out_rows = pl.ds(rstep * nblk + i, Q_BLOCK, stride=dilation)
    sel = 1 if has_prev is True else jnp.where(has_prev, 1, 0)
    def keys_values(ref_prev, ref_cur, hs):
        if dilation == 1 and i > 0:
            return ref_cur[(i - 1) * Q_BLOCK:(i + 1) * Q_BLOCK, hs]
        prev_rows = slice(0, Q_BLOCK) if dilation == 1 else rows
        return jnp.concatenate([ref_prev[prev_rows, hs], ref_cur[rows, hs]], axis=0)

    def scores(h):
        hs = slice(h * HEAD_DIM, (h + 1) * HEAD_DIM)
        return lax.dot_general(q_ref[rows, hs], keys_values(kp_ref, kc_ref, hs), (((1,), (1,)), ((), ())),
                               preferred_element_type=F32)

    lse = jnp.zeros((Q_BLOCK, HEAD_DIM), F32)
    s_next = scores(0)
    for h in range(HEADS_PER_GROUP):
        s = s_next
        if h + 1 < HEADS_PER_GROUP:
            s_next = scores(h + 1)
        if filler is not None:
            filler(h)
        s = s * scale + bias_ref[sel, h]
        m = jnp.max(s, axis=-1, keepdims=True)
        p = jnp.exp(s - m)
        den = jnp.sum(p, axis=-1, keepdims=True)
        vcat = keys_values(vp_ref, vc_ref, slice(h * HEAD_DIM, (h + 1) * HEAD_DIM))
        o = jnp.dot(p.astype(BF16), vcat, preferred_element_type=F32) / den
        o_ref[h, out_rows, :] = o
        lse = jnp.where(lane == h, m + jnp.log(den), lse)
    l_ref[out_rows, :] = lse


def _attn_plan(qkv, *, batch, seq, window, dilation, slopes, nblk):
    M, width = qkv.shape
    gw = HEADS_PER_GROUP * HEAD_DIM
    assert width == 3 * gw and seq % (dilation * Q_BLOCK) == 0
    nb = seq // (dilation * Q_BLOCK)
    blocks_per_batch = nb * dilation
    bias = jnp.asarray(_attn_bias(slopes, dilation, window // dilation))
    blk = (nblk * Q_BLOCK, gw)
    if dilation == 1:
        assert nb % nblk == 0
        grid = (batch, nb // nblk, 1)
        first = lambda b, n, r: b * blocks_per_batch + n * nblk
        cur = lambda sec: (lambda b, n, r: (first(b, n, r) // nblk, sec))
        prev = lambda sec: (lambda b, n, r: (jnp.maximum(first(b, n, r) - 1, 0), sec))
        prev_blk = (Q_BLOCK, gw)
        out_rows = nblk * Q_BLOCK
        out_idx = lambda b, n, r: b * (nb // nblk) + n
    else:
        assert dilation % nblk == 0
        grid = (batch, nb, dilation // nblk)
        first = lambda b, n, r: b * blocks_per_batch + n * dilation + r * nblk
        cur = lambda sec: (lambda b, n, r: (first(b, n, r) // nblk, sec))
        prev = lambda sec: (lambda b, n, r: (first(b, jnp.maximum(n - 1, 0), r) // nblk, sec))
        prev_blk = blk
        out_rows = dilation * Q_BLOCK
        out_idx = lambda b, n, r: b * nb + n
    return dict(
        grid=grid,
        operands=(qkv, qkv, qkv, qkv, qkv, bias),
        in_blocks=[blk, blk, prev_blk, blk, prev_blk, bias.shape],
        in_maps=[cur(0), cur(1), prev(1), cur(2), prev(2), lambda b, n, r: (0, 0, 0, 0)],
        out_blocks=[(HEADS_PER_GROUP, out_rows, HEAD_DIM), (out_rows, HEAD_DIM)],
        out_maps=[lambda b, n, r: (0, out_idx(b, n, r), 0), lambda b, n, r: (out_idx(b, n, r), 0)],
        out_shape=[jax.ShapeDtypeStruct((HEADS_PER_GROUP, M, HEAD_DIM), F32), jax.ShapeDtypeStruct((M, HEAD_DIM), F32)],
        vmem=(2 * (3 * nblk * Q_BLOCK + 2 * prev_blk[0]) * gw * 2 + 2 * out_rows * (gw + HEAD_DIM) * 4
              + 2 * bias.size * 4 + (4 << 20)),
    )


def _sigmoid(x):
    return 0.5 * (jnp.tanh(0.5 * x) + 1.0)


def _host_body(*refs, dilation, act, att_dilation, nblk, att_grid):
    if dilation > 1:
        a_ref, p_ref, w_ref, *att_in, o_ref, ao_ref, al_ref, asub_ref = refs
    else:
        a_ref, w_ref, *att_in, o_ref, ao_ref, al_ref = refs
        asub_ref = a_ref
    i, j = pl.program_id(0), pl.program_id(1)

    if dilation > 1:
        @pl.when(j == 0)
        def _():
            _reorder_rows(a_ref, p_ref, asub_ref, dilation)

    _, gn, gr = att_grid
    step = i * pl.num_programs(1) + j
    n, rstep = (step // gr) % gn, step % gr
    rc = a_ref.shape[0] // nblk
    ncol = o_ref.shape[1] // V7X_MXU_DIM
    every = -(-HEADS_PER_GROUP // ncol)
    for b in range(nblk):
        rows = slice(b * rc, (b + 1) * rc)

        def project_piece(h, rows=rows):
            if h % every == 0 and h // every < ncol:
                cols = slice(h // every * V7X_MXU_DIM, (h // every + 1) * V7X_MXU_DIM)
                acc = jnp.dot(asub_ref[rows, :], w_ref[:, cols], preferred_element_type=F32)
                o_ref[rows, cols] = (_sigmoid(acc) if act == "sigmoid" else acc).astype(o_ref.dtype)

        _attn_block(b, n, rstep, *att_in, ao_ref, al_ref, dilation=att_dilation, nblk=nblk, filler=project_piece)


def _project_hosting_attention(a, w, cols, qkv_att, *, tm, tn, name, att_dilation, nblk, dilation=1, act=None,
                               **att_kw):
    M, K = a.shape
    col0, N = cols
    nj = N // tn
    assert M % tm == 0 and N % tn == 0 and col0 % tn == 0 and tm % (dilation * Q_BLOCK) == 0
    assert tn // V7X_MXU_DIM <= HEADS_PER_GROUP
    plan = _attn_plan(qkv_att, dilation=att_dilation, nblk=nblk, **att_kw)
    gb, gn, gr = plan["grid"]
    assert (M // tm) * nj == gb * gn * gr

    def hosted(index_map):
        def host_map(i, j):
            step = i * nj + j
            return index_map(step // (gn * gr), (step // gr) % gn, step % gr)
        return host_map

    in_specs = [pl.BlockSpec((tm, K), lambda i, j: (i, 0))]
    operands = [a]
    scratch = []
    vmem = 2 * (tm * K * 2 + K * tn * 2 + tm * tn * 2) + (tm // nblk) * tn * 4 + plan["vmem"]
    if dilation > 1:
        perm = jnp.asarray(_subseq_perm(dilation), BF16)
        in_specs.append(pl.BlockSpec(perm.shape, lambda i, j: (0, 0)))
        operands.append(perm)
        scratch = [pltpu.VMEM((tm, K), BF16)]
        vmem += tm * K * 2
    in_specs.append(pl.BlockSpec((K, tn), lambda i, j: (0, j + col0 // tn)))
    outs = pl.pallas_call(
        functools.partial(_host_body, dilation=dilation, act=act, att_dilation=att_dilation, nblk=nblk,
                          att_grid=plan["grid"]),
        grid=(M // tm, nj),
        in_specs=in_specs + [pl.BlockSpec(b, hosted(m)) for b, m in zip(plan["in_blocks"], plan["in_maps"])],
        out_specs=[pl.BlockSpec((tm, tn), lambda i, j: (i, j))]
                  + [pl.BlockSpec(b, hosted(m)) for b, m in zip(plan["out_blocks"], plan["out_maps"])],
        out_shape=[jax.ShapeDtypeStruct((M, N), BF16)] + plan["out_shape"],
        scratch_shapes=scratch,
        compiler_params=_params(2, vmem),
        name=name,
    )(*operands, w, *plan["operands"])
    return outs[0], outs[1:]


def _mix_body(y_ref, o1_ref, l1_ref, o2_ref, l2_ref, o3_ref, l3_ref, gr_ref, ga_ref, wr_ref, wa_ref,
              out_ref, *, row_chunk):
    for c in range(y_ref.shape[0] // row_chunk):
        rows = slice(c * row_chunk, (c + 1) * row_chunk)
        l1, l2, l3 = l1_ref[rows, :], l2_ref[rows, :], l3_ref[rows, :]
        m = jnp.maximum(jnp.maximum(l1, l2), l3)
        e1, e2, e3 = jnp.exp(l1 - m), jnp.exp(l2 - m), jnp.exp(l3 - m)
        den = e1 + e2 + e3
        w1, w2, w3 = e1 / den, e2 / den, e3 / den
        heads = []
        for h in range(HEADS_PER_GROUP):
            bc = lambda w: jnp.broadcast_to(w[:, h:h + 1], (row_chunk, HEAD_DIM))
            att = bc(w1) * o1_ref[h, rows, :] + bc(w2) * o2_ref[h, rows, :] + bc(w3) * o3_ref[h, rows, :]
            heads.append(att.astype(BF16))
        att = jnp.concatenate(heads, axis=1)
        ya = jnp.dot(y_ref[rows, :], wr_ref[...], preferred_element_type=F32)
        yb = jnp.dot(att, wa_ref[...], preferred_element_type=F32)
        out_ref[rows, :] = (gr_ref[rows, :].astype(F32) * ya + ga_ref[rows, :].astype(F32) * yb).astype(out_ref.dtype)


def _mix(y_rec, att_parts, gates, w_rnn, w_att, *, tm, row_chunk):
    M, d_rnn = y_rec.shape
    aw, D = w_att.shape
    rowblk = lambda w: pl.BlockSpec((tm, w), lambda i: (i, 0))
    headblk = pl.BlockSpec((HEADS_PER_GROUP, tm, HEAD_DIM), lambda i: (0, i, 0))
    resident = lambda r: pl.BlockSpec((r, D), lambda i: (0, 0), pipeline_mode=pl.Buffered(1))
    vmem = (2 * (tm * d_rnn * 2 + 3 * tm * (aw + HEAD_DIM) * 4 + 2 * tm * D * 2 + tm * D * 2) + (d_rnn + aw) * D * 2
            + 4 * row_chunk * D * 4)
    return pl.pallas_call(
        functools.partial(_mix_body, row_chunk=row_chunk),
        grid=(M // tm,),
        in_specs=[rowblk(d_rnn)] + [headblk, rowblk(HEAD_DIM)] * 3 + [
            pl.BlockSpec((tm, D), lambda i: (i, 0)),
            pl.BlockSpec((tm, D), lambda i: (i, 1)),
            resident(d_rnn), resident(aw)],
        out_specs=rowblk(D),
        out_shape=jax.ShapeDtypeStruct((M, D), BF16),
        compiler_params=_params(1, vmem),
        name="mix",
    )(y_rec, *att_parts, gates, gates, w_rnn, w_att)


def _outproj_body(m_ref, w_ref, x_ref, g_ref, x1_ref, h2_ref, *, row_chunk):
    for c in range(m_ref.shape[0] // row_chunk):
        rows = slice(c * row_chunk, (c + 1) * row_chunk)
        x1 = x_ref[rows, :] + jnp.dot(m_ref[rows, :], w_ref[...], preferred_element_type=F32)
        x1_ref[rows, :] = x1
        h2_ref[rows, :] = _rms(x1, g_ref[...]).astype(h2_ref.dtype)


def _outproj(mixed, w_out, x, g2, *, tm, row_chunk):
    M, D = x.shape
    vmem = 2 * (tm * D * 2 + D * D * 2 + tm * D * 4 + tm * D * 4 + tm * D * 2) + 2 * tm * D * 4
    return pl.pallas_call(
        functools.partial(_outproj_body, row_chunk=row_chunk),
        grid=(M // tm,),
        in_specs=[pl.BlockSpec((tm, D), lambda i: (i, 0)), pl.BlockSpec((D, D), lambda i: (0, 0)),
                  pl.BlockSpec((tm, D), lambda i: (i, 0)), pl.BlockSpec((1, D), lambda i: (0, 0))],
        out_specs=[pl.BlockSpec((tm, D), lambda i: (i, 0))] * 2,
        out_shape=[jax.ShapeDtypeStruct((M, D), F32), jax.ShapeDtypeStruct((M, D), BF16)],
        compiler_params=_params(1, vmem),
        name="outproj",
    )(mixed, w_out, x, g2)


def _ffn_up_body(h_ref, w_ref, cw_ref, cb_ref, o_ref, gs_ref, vs_ref, halo_ref, *, tiles_per_batch, row_chunk):
    i = pl.program_id(0)
    j = pl.program_id(1)
    tm, tn = o_ref.shape

    @pl.when(i % tiles_per_batch == 0)
    def _():
        gs_ref[0:V7X_SUBLANES, :] = jnp.zeros((V7X_SUBLANES, tn), F32)

    @pl.when(i % tiles_per_batch != 0)
    def _():
        gs_ref[0:V7X_SUBLANES, :] = halo_ref[j]

    for c in range(tm // row_chunk):
        rows = slice(c * row_chunk, (c + 1) * row_chunk)
        base = V7X_SUBLANES + c * row_chunk
        h = h_ref[rows, :]
        gv = jnp.dot(h, w_ref[...], preferred_element_type=F32)
        gs_ref[base:base + row_chunk, :] = gv[:, :tn]
        vs_ref[rows, :] = gv[:, tn:]
        for s in range(tn // V7X_LANES):
            cs = slice(s * V7X_LANES, (s + 1) * V7X_LANES)
            gate = cb_ref[:, cs] + cw_ref[CONV_FFN - 1:CONV_FFN, cs] * gs_ref[base:base + row_chunk, cs]
            for t in range(CONV_FFN - 1):
                r0 = base - (CONV_FFN - 1) + t
                gate = gate + cw_ref[t:t + 1, cs] * gs_ref[r0:r0 + row_chunk, cs]
            o_ref[rows, cs] = (jax.nn.gelu(gate) * vs_ref[rows, cs]).astype(o_ref.dtype)
    halo_ref[j] = gs_ref[tm:tm + V7X_SUBLANES, :]


def _ffn_up(h2, w_up, conv_w, conv_b, *, seq, tm, tn, row_chunk):
    M, D = h2.shape
    d_ff = w_up.shape[1] // 2
    nj = d_ff // tn
    w_pack = w_up.reshape(D, 2, nj, tn).transpose(0, 2, 1, 3).reshape(D, 2 * d_ff)
    vmem = (2 * (tm * D * 2 + 2 * D * tn * 2 + tm * tn * 2) + (tm + 8) * tn * 4 + nj * 8 * tn * 4
            + 6 * row_chunk * tn * 4)
    return pl.pallas_call(
        functools.partial(_ffn_up_body, tiles_per_batch=seq // tm, row_chunk=row_chunk),
        grid=(M // tm, nj),
        in_specs=[pl.BlockSpec((tm, D), lambda i, j: (i, 0)),
                  pl.BlockSpec((D, 2 * tn), lambda i, j: (0, j)),
                  pl.BlockSpec((CONV_FFN, tn), lambda i, j: (0, j)),
                  pl.BlockSpec((1, tn), lambda i, j: (0, j))],
        out_specs=pl.BlockSpec((tm, tn), lambda i, j: (i, j)),
        out_shape=jax.ShapeDtypeStruct((M, d_ff), BF16),
        scratch_shapes=[pltpu.VMEM((tm + V7X_SUBLANES, tn), F32),
                        pltpu.VMEM((tm, tn), F32),
                        pltpu.VMEM((nj, V7X_SUBLANES, tn), F32)],
        compiler_params=_params(2, vmem),
        name="ffn_up",
    )(h2, w_pack, conv_w, conv_b)


def _ffn_down_body(a_ref, w_ref, x_ref, g_ref, o_ref, *, final_norm, row_chunk):
    for c in range(a_ref.shape[0] // row_chunk):
        rows = slice(c * row_chunk, (c + 1) * row_chunk)
        x2 = x_ref[rows, :] + jnp.dot(a_ref[rows, :], w_ref[...], preferred_element_type=F32)
        o_ref[rows, :] = _rms(x2, g_ref[...]) if final_norm else x2


def _ffn_down(a, w_down, x1, g, *, tm, row_chunk, final_norm):
    M, d_ff = a.shape
    D = w_down.shape[1]
    vmem = 2 * (tm * d_ff * 2 + 2 * tm * D * 4) + d_ff * D * 2 + 2 * tm * D * 4
    return pl.pallas_call(
        functools.partial(_ffn_down_body, final_norm=final_norm, row_chunk=row_chunk),
        grid=(M // tm,),
        in_specs=[pl.BlockSpec((tm, d_ff), lambda i: (i, 0)),
                  pl.BlockSpec((d_ff, D), lambda i: (0, 0), pipeline_mode=pl.Buffered(1)),
                  pl.BlockSpec((tm, D), lambda i: (i, 0)),
                  pl.BlockSpec((1, D), lambda i: (0, 0))],
        out_specs=pl.BlockSpec((tm, D), lambda i: (i, 0)),
        out_shape=jax.ShapeDtypeStruct((M, D), F32),
        compiler_params=_params(1, vmem),
        name="ffn_down",
    )(a, w_down, x1, g)


def _layer(x2d, p, *, batch, seq):
    d_rnn = p["conv_w"].shape[1]
    gw = HEADS_PER_GROUP * HEAD_DIM
    att_w = len(ATT_GROUPS) * gw
    off_q = 2 * d_rnn
    off_g = off_q + 3 * att_w
    qkv_tn, gates_tn = 768, 1024
    w_f32 = p["w_in"]
    sections = [w_f32[:, :off_q]]
    for g in range(len(ATT_GROUPS)):
        sections += [w_f32[:, off_q + sec * att_w + g * gw: off_q + sec * att_w + (g + 1) * gw] for sec in range(3)]
    qkv_col0 = [off_q + g * 3 * gw for g in range(len(ATT_GROUPS))]
    gates_col0 = -(-off_g // gates_tn) * gates_tn
    sections += [jnp.zeros((w_f32.shape[0], gates_col0 - off_g), w_f32.dtype), w_f32[:, off_g:]]
    w_in = jnp.concatenate(sections, axis=1).astype(BF16)

    band = _lru_band(d_rnn, d_rnn // LRU_BLOCKS)
    y_rec, h = _lru(x2d, p["norm1_g"][None, :], w_in, p["conv_w"], p["conv_b"][None, :],
                    _band_tiles(p["lru_wr"], band), _band_tiles(p["lru_wi"], band), p["lru_br"][None, :],
                    p["lru_bi"][None, :], p["lru_lambda"][None, :], seq=seq, ts=256, tiles=2, band=band)

    slopes = _alibi_slopes(len(ATT_GROUPS) * HEADS_PER_GROUP).reshape(len(ATT_GROUPS), HEADS_PER_GROUP)
    att_parts, qkv_prev, att_prev = [], None, None
    for g, (window, dilation) in enumerate(ATT_GROUPS):
        cols = (qkv_col0[g], 3 * gw)
        if qkv_prev is None:
            qkv = _matmul(h, w_in, cols, tm=2048, tn=qkv_tn, name=f"inproj_qkv_d{dilation}")
        else:
            qkv, parts = _project_hosting_attention(
                h, w_in, cols, qkv_prev, tm=2048, tn=qkv_tn, dilation=dilation, nblk=4,
                name=f"inproj_qkv_d{dilation}_attn_d{att_prev['dilation']}", batch=batch, seq=seq,
                att_dilation=att_prev["dilation"], window=att_prev["window"], slopes=att_prev["slopes"])
            att_parts += parts
        qkv_prev, att_prev = qkv, dict(window=window, dilation=dilation, slopes=slopes[g])
    gates, parts = _project_hosting_attention(
        h, w_in, (gates_col0, w_f32.shape[1] - off_g), qkv_prev, tm=1024, tn=gates_tn, act="sigmoid", nblk=2,
        name=f"inproj_gates_attn_d{att_prev['dilation']}", batch=batch, seq=seq,
        att_dilation=att_prev["dilation"], window=att_prev["window"], slopes=att_prev["slopes"])
    att_parts += parts

    mixed = _mix(y_rec, att_parts, gates, p["w_rnn_out"].astype(BF16), p["w_att_out"].astype(BF16),
                 tm=512, row_chunk=512)
    x1, h2 = _outproj(mixed, p["w_out"].astype(BF16), x2d, p["norm2_g"][None, :], tm=512, row_chunk=512)
    hmid = _ffn_up(h2, p["w_up"].astype(BF16), p["ffn_conv_w"], p["ffn_conv_b"][None, :], seq=seq,
                   tm=2048, tn=512, row_chunk=1024)
    return x1, hmid


def kernel(x, norm1_g, w_in, conv_w, conv_b, lru_wr, lru_br, lru_wi, lru_bi, lru_lambda, w_rnn_out, w_att_out, w_out,
           norm2_g, w_up, ffn_conv_w, ffn_conv_b, w_down, final_g):
    B, S, D = x.shape
    depth = w_in.shape[0]
    x2d = x.reshape(B * S, D)
    for l in range(depth):
        p = dict(norm1_g=norm1_g[l], w_in=w_in[l], conv_w=conv_w[l], conv_b=conv_b[l], lru_wr=lru_wr[l],
                 lru_br=lru_br[l], lru_wi=lru_wi[l], lru_bi=lru_bi[l], lru_lambda=lru_lambda[l],
                 w_rnn_out=w_rnn_out[l], w_att_out=w_att_out[l], w_out=w_out[l], norm2_g=norm2_g[l],
                 w_up=w_up[l], ffn_conv_w=ffn_conv_w[l], ffn_conv_b=ffn_conv_b[l])
        x1, hmid = _layer(x2d, p, batch=B, seq=S)
        x2d = _ffn_down(hmid, w_down[l].astype(BF16), x1, final_g[None, :], tm=512, row_chunk=512,
                        final_norm=l == depth - 1)
    return x2d.reshape(B, S, D)
```
